```python
import math
import jax, jax.numpy as jnp
from jax import lax
import numpy as np

D_MODEL = 1024
BATCH = 16
SEQ = 2048
DEPTH = 1

MIX_WIDTH = D_MODEL
RWKV_WIDTH = MIX_WIDTH // 2
RWKV_HEAD = 64
RWKV_HEADS = RWKV_WIDTH // RWKV_HEAD
DECAY_LORA = 64
ICL_LORA = 64
GATE_LORA = 128
GN_EPS = 64e-5
DIFF_WIDTH = MIX_WIDTH - RWKV_WIDTH
DIFF_HEAD = 64
DIFF_HEADS = DIFF_WIDTH // (2 * DIFF_HEAD)
DIFF_VDIM = 2 * DIFF_HEAD
Q_BLOCK = 128
RWKV_COLS = 3 * RWKV_WIDTH + DECAY_LORA + ICL_LORA + GATE_LORA
DIFF_COLS = 3 * DIFF_WIDTH
IN_COLS = RWKV_COLS + DIFF_COLS
N_EXPERTS = 32
TOP_K = 4
D_FF = D_MODEL
SWIGLU_ALPHA = 1.702
SWIGLU_LIMIT = 7.0
MOE_BLOCK = 128
NORM_EPS = 1e-5

kernel_name = 'hymba_rwkv7_diffattn_moe_layer'


def _rmsnorm(x, g, eps=NORM_EPS):
    xf = x.astype(jnp.float32)
    y = xf * lax.rsqrt(jnp.mean(xf * xf, axis=-1, keepdims=True) + eps)
    return (y * g.astype(jnp.float32)).astype(x.dtype)


def _alibi_slopes(n_heads):
    return 2.0 ** (-8.0 * (jnp.arange(n_heads, dtype=jnp.float32) + 1.0) / n_heads)


def _rwkv7_group(p, mu, w0, w_b, a0, a_b, g_b, k_k, k_a, r_k, ln_g, ln_b):
    bsz, seq, _ = p.shape
    f32 = jnp.float32
    prev = jnp.pad(p[:, :-1], ((0, 0), (1, 0), (0, 0)))
    p = p + (prev - p) * mu
    o1, o2, o3 = RWKV_WIDTH, 2 * RWKV_WIDTH, 3 * RWKV_WIDTH
    o4 = o3 + DECAY_LORA
    o5 = o4 + ICL_LORA
    r, k, v = p[..., :o1], p[..., o1:o2], p[..., o2:o3]
    w_lo, a_lo, g_lo = p[..., o3:o4], p[..., o4:o5], p[..., o5:]
    w = -jax.nn.softplus(-(w0 + jnp.tanh(w_lo) @ w_b).astype(f32)) - 0.5
    decay = jnp.exp(-jnp.exp(w))
    a = jax.nn.sigmoid(a0 + a_lo @ a_b)
    g = jax.nn.sigmoid(g_lo) @ g_b
    heads = lambda t: t.reshape(bsz, seq, RWKV_HEADS, RWKV_HEAD).astype(f32)
    kk = heads(k * k_k)
    kk = kk / jnp.maximum(jnp.sqrt(jnp.sum(kk * kk, axis=-1, keepdims=True)), 1e-12)
    k = k * (1.0 + (a - 1.0) * k_a)
    r_h, k_h, v_h, a_h, w_h = heads(r), heads(k), heads(v), heads(a), heads(decay)
    b_h = kk * a_h

    def step(state, inp):
        r_t, w_t, k_t, v_t, kk_t, b_t = inp
        sa = jnp.einsum('bhvk,bhk->bhv', state, kk_t)
        state = (state * w_t[:, :, None, :]
                 - sa[..., None] * b_t[:, :, None, :]
                 + v_t[..., None] * k_t[:, :, None, :])
        y_t = jnp.einsum('bhvk,bhk->bhv', state, r_t)
        return state, y_t

    xs = tuple(jnp.swapaxes(t, 0, 1) for t in (r_h, w_h, k_h, v_h, kk, b_h))
    state0 = jnp.zeros((bsz, RWKV_HEADS, RWKV_HEAD, RWKV_HEAD), f32)
    _, y = lax.scan(step, state0, xs)
    y = jnp.swapaxes(y, 0, 1)
    mean = jnp.mean(y, axis=-1, keepdims=True)
    var = jnp.mean(jnp.square(y - mean), axis=-1, keepdims=True)
    y = ((y - mean) * lax.rsqrt(var + GN_EPS)).reshape(bsz, seq, RWKV_WIDTH)
    y = y * ln_g.astype(f32) + ln_b.astype(f32)
    bonus = jnp.sum(r_h * k_h * r_k.astype(f32), axis=-1, keepdims=True) * v_h
    y = (y + bonus.reshape(bsz, seq, RWKV_WIDTH)) * g.astype(f32)
    return y.astype(p.dtype)


def _diff_attn_group(p, lq1, lk1, lq2, lk2, subln_g, lambda_init):
    bsz, seq, _ = p.shape
    f32 = jnp.float32
    q = p[..., :DIFF_WIDTH].reshape(bsz, seq, DIFF_HEADS, 2, DIFF_HEAD)
    k = p[..., DIFF_WIDTH:2 * DIFF_WIDTH].reshape(bsz, seq, DIFF_HEADS, 2, DIFF_HEAD)
    v = p[..., 2 * DIFF_WIDTH:].reshape(bsz, seq, DIFF_HEADS, DIFF_VDIM)
    lam = (jnp.exp(jnp.sum(lq1.astype(f32) * lk1.astype(f32)))
           - jnp.exp(jnp.sum(lq2.astype(f32) * lk2.astype(f32))) + lambda_init)
    slopes = _alibi_slopes(DIFF_HEADS)
    scale = DIFF_HEAD ** -0.5
    n_blocks = seq // Q_BLOCK
    q_blocks = jnp.swapaxes(q.reshape(bsz, n_blocks, Q_BLOCK, DIFF_HEADS, 2, DIFF_HEAD), 0, 1)
    k_pos = jnp.arange(seq)

    def block(args):
        q_blk, i = args
        s = jnp.einsum('bqhcd,bkhcd->bhcqk', q_blk, k).astype(f32) * scale
        q_pos = i * Q_BLOCK + jnp.arange(Q_BLOCK)
        dist = (q_pos[:, None] - k_pos[None, :]).astype(f32)
        s = s - slopes[None, :, None, None, None] * dist
        s = jnp.where(dist >= 0, s, -jnp.inf)
        pr = jax.nn.softmax(s, axis=-1)
        attn = pr[:, :, 0] - lam * pr[:, :, 1]
        return jnp.einsum('bhqk,bkhd->bqhd', attn.astype(v.dtype), v)

    o = lax.map(block, (q_blocks, jnp.arange(n_blocks)))
    o = jnp.swapaxes(o, 0, 1).reshape(bsz, seq, DIFF_HEADS, DIFF_VDIM)
    o = _rmsnorm(o, subln_g) * (1.0 - lambda_init)
    return o.reshape(bsz, seq, DIFF_WIDTH).astype(p.dtype)


def _moe(xn, router_w, router_b, w_up, b_up, w_down, b_down):
    bsz, seq, d = xn.shape
    xt = xn.reshape(-1, d)
    n_tok = xt.shape[0]
    logits = xt.astype(jnp.float32) @ router_w.astype(jnp.float32) + router_b.astype(jnp.float32)
    top_v, top_i = lax.top_k(logits, TOP_K)
    gate = jax.nn.softmax(top_v, axis=-1).astype(xt.dtype)
    n_assign = n_tok * TOP_K
    e_flat = top_i.reshape(n_assign)
    order = jnp.argsort(e_flat)
    e_sorted = e_flat[order]
    tok_sorted = (order // TOP_K).astype(jnp.int32)
    gate_sorted = gate.reshape(n_assign)[order]
    counts = jnp.bincount(e_flat, length=N_EXPERTS)
    start = jnp.cumsum(counts) - counts
    padded = (counts + MOE_BLOCK - 1) // MOE_BLOCK * MOE_BLOCK
    pad_end = jnp.cumsum(padded)
    pad_start = pad_end - padded
    dest = pad_start[e_sorted] + (jnp.arange(n_assign) - start[e_sorted])
    n_blocks = -(-n_assign // MOE_BLOCK) + N_EXPERTS
    n_slots = n_blocks * MOE_BLOCK
    slot_tok = jnp.zeros((n_slots,), jnp.int32).at[dest].set(tok_sorted)
    slot_gate = jnp.zeros((n_slots,), gate.dtype).at[dest].set(gate_sorted)
    blk_expert = jnp.minimum(
        jnp.searchsorted(pad_end, jnp.arange(n_blocks) * MOE_BLOCK, side='right'), N_EXPERTS - 1)

    def block(args):
        toks, gts, e = args
        h = xt[toks] @ w_up[e] + b_up[e]
        glu = jnp.minimum(h[:, ::2], SWIGLU_LIMIT)
        lin = jnp.clip(h[:, 1::2], -SWIGLU_LIMIT, SWIGLU_LIMIT)
        act = glu * jax.nn.sigmoid(SWIGLU_ALPHA * glu) * (lin + 1.0)
        y = act @ w_down[e] + b_down[e]
        return y * gts[:, None]

    ys = lax.map(block, (slot_tok.reshape(n_blocks, MOE_BLOCK),
                         slot_gate.reshape(n_blocks, MOE_BLOCK), blk_expert))
    out = jax.ops.segment_sum(ys.reshape(n_slots, d), slot_tok, num_segments=n_tok)
    return out.reshape(bsz, seq, d).astype(xn.dtype)


def setup_inputs(seed: int = 0) -> dict:
    key = jax.random.key(seed)
    ks = jax.random.split(key, 32)
    L, D, W = DEPTH, D_MODEL, RWKV_WIDTH
    nrm = lambda k, shape, s: jax.random.normal(k, shape, jnp.float32) * s
    return {
        'x': nrm(ks[0], (BATCH, SEQ, D), 1.0),
        'norm1_g': 1.0 + nrm(ks[1], (L, D), 0.05),
        'w_in': nrm(ks[2], (L, D, IN_COLS), D ** -0.5),
        'rwkv_mu': jax.random.uniform(ks[3], (L, RWKV_COLS), jnp.float32),
        'rwkv_w0': jax.random.uniform(ks[4], (L, W), jnp.float32, -6.0, 1.0),
        'rwkv_wb': nrm(ks[5], (L, DECAY_LORA, W), 0.5 * DECAY_LORA ** -0.5),
        'rwkv_a0': nrm(ks[6], (L, W), 0.1),
        'rwkv_ab': nrm(ks[7], (L, ICL_LORA, W), 0.5 * ICL_LORA ** -0.5),
        'rwkv_gb': nrm(ks[8], (L, GATE_LORA, W), GATE_LORA ** -0.5),
        'rwkv_kk': 0.85 + nrm(ks[9], (L, W), 0.05),
        'rwkv_ka': 1.0 + nrm(ks[10], (L, W), 0.05),
        'rwkv_rk': nrm(ks[11], (L, RWKV_HEADS, RWKV_HEAD), 0.1),
        'rwkv_ln_g': 1.0 + nrm(ks[12], (L, W), 0.05),
        'rwkv_ln_b': nrm(ks[13], (L, W), 0.01),
        'diff_lq1': nrm(ks[14], (L, DIFF_HEAD), 0.1),
        'diff_lk1': nrm(ks[15], (L, DIFF_HEAD), 0.1),
        'diff_lq2': nrm(ks[16], (L, DIFF_HEAD), 0.1),
        'diff_lk2': nrm(ks[17], (L, DIFF_HEAD), 0.1),
        'diff_subln_g': 1.0 + nrm(ks[18], (L, DIFF_VDIM), 0.05),
        'w_out': nrm(ks[19], (L, MIX_WIDTH, D), MIX_WIDTH ** -0.5),
        'norm2_g': 1.0 + nrm(ks[20], (L, D), 0.05),
        'router_w': nrm(ks[21], (L, D, N_EXPERTS), D ** -0.5),
        'router_b': nrm(ks[22], (L, N_EXPERTS), 0.01),
        'exp_w_up': nrm(ks[23], (L, N_EXPERTS, D, 2 * D_FF), D ** -0.5),
        'exp_b_up': nrm(ks[24], (L, N_EXPERTS, 2 * D_FF), 0.01),
        'exp_w_down': nrm(ks[25], (L, N_EXPERTS, D_FF, D), D_FF ** -0.5),
        'exp_b_down': nrm(ks[26], (L, N_EXPERTS, D), 0.01),
        'final_g': 1.0 + nrm(ks[27], (D,), 0.05),
    }


def reference(x, norm1_g, w_in, rwkv_mu, rwkv_w0, rwkv_wb, rwkv_a0, rwkv_ab, rwkv_gb,
              rwkv_kk, rwkv_ka, rwkv_rk, rwkv_ln_g, rwkv_ln_b, diff_lq1, diff_lk1,
              diff_lq2, diff_lk2, diff_subln_g, w_out, norm2_g, router_w, router_b,
              exp_w_up, exp_b_up, exp_w_down, exp_b_down, final_g):
    h = x
    for l in range(DEPTH):
        lambda_init = 0.8 - 0.6 * math.exp(-0.3 * l)
        xn = _rmsnorm(h, norm1_g[l])
        proj = xn @ w_in[l]
        y_rwkv = _rwkv7_group(proj[..., :RWKV_COLS], rwkv_mu[l], rwkv_w0[l], rwkv_wb[l],
                              rwkv_a0[l], rwkv_ab[l], rwkv_gb[l], rwkv_kk[l], rwkv_ka[l],
                              rwkv_rk[l], rwkv_ln_g[l], rwkv_ln_b[l])
        y_diff = _diff_attn_group(proj[..., RWKV_COLS:], diff_lq1[l], diff_lk1[l],
                                  diff_lq2[l], diff_lk2[l], diff_subln_g[l], lambda_init)
        h = h + jnp.concatenate([y_rwkv, y_diff], axis=-1) @ w_out[l]
        h = h + _moe(_rmsnorm(h, norm2_g[l]), router_w[l], router_b[l],
                     exp_w_up[l], exp_b_up[l], exp_w_down[l], exp_b_down[l])
    return _rmsnorm(h, final_g)
```

```python
import functools
import math

import jax
import jax.numpy as jnp
from jax import lax
from jax.experimental import pallas as pl
from jax.experimental.pallas import tpu as pltpu

F32 = jnp.float32
BF16 = jnp.bfloat16

D_MODEL = 1024
RWKV_WIDTH = 512
RWKV_HEAD = 64
DECAY_LORA = 64
ICL_LORA = 64
GATE_LORA = 128
GN_EPS = 64e-5
DIFF_WIDTH = 512
DIFF_HEAD = 64
DIFF_HEADS = 4
DIFF_VDIM = 128
RWKV_COLS = 3 * RWKV_WIDTH + DECAY_LORA + ICL_LORA + GATE_LORA
IN_COLS = RWKV_COLS + 3 * DIFF_WIDTH
N_EXPERTS = 32
TOP_K = 4
D_FF = 1024
SWIGLU_ALPHA = 1.702
SWIGLU_LIMIT = 7.0
NORM_EPS = 1e-5
LAMBDA_INIT = 0.8 - 0.6 * math.exp(-0.0)

LANES = 128
VMEM_LIMIT_BYTES = 56 * 1024 * 1024

ROW_TILE = 512
RWKV_CHUNK = 64
RWKV_GROUP = 4
GROUP_LANES = RWKV_GROUP * RWKV_HEAD


def _dot(a, b):
    return jnp.dot(a, b, preferred_element_type=F32)


def _dot_nt(a, b):
    return lax.dot_general(a, b, (((1,), (1,)), ((), ())), preferred_element_type=F32)


def _dot_tn(a, b):
    return lax.dot_general(a, b, (((0,), (0,)), ((), ())), preferred_element_type=F32)


def _split2(x):
    hi = x.astype(BF16)
    lo = (x - hi.astype(F32)).astype(BF16)
    return hi, lo


def _split3(x):
    hi = x.astype(BF16)
    r1 = x - hi.astype(F32)
    mid = r1.astype(BF16)
    lo = (r1 - mid.astype(F32)).astype(BF16)
    return hi, mid, lo


def _rms(x, g):
    return x * lax.rsqrt(jnp.mean(x * x, axis=-1, keepdims=True) + NORM_EPS) * g


def _inproj_kernel(x_ref, g_ref, w_ref, o_ref):
    xn = _rms(x_ref[...], g_ref[...])
    o_ref[...] = _dot(xn.astype(BF16), w_ref[...])


def _inproj(x2d, g, w_bf16):
    n_tok = x2d.shape[0]
    return pl.pallas_call(
        _inproj_kernel,
        grid=(n_tok // ROW_TILE,),
        in_specs=[
            pl.BlockSpec((ROW_TILE, D_MODEL), lambda i: (i, 0)),
            pl.BlockSpec((1, D_MODEL), lambda i: (0, 0)),
            pl.BlockSpec((D_MODEL, IN_COLS), lambda i: (0, 0)),
        ],
        out_specs=pl.BlockSpec((ROW_TILE, IN_COLS), lambda i: (i, 0)),
        out_shape=jax.ShapeDtypeStruct((n_tok, IN_COLS), F32),
        compiler_params=pltpu.CompilerParams(
            dimension_semantics=("arbitrary",), vmem_limit_bytes=VMEM_LIMIT_BYTES),
        name="inproj",
    )(x2d, g, w_bf16)


def _head_sum(x, bd_ones):
    hi, lo = _split2(x)
    return _dot(hi, bd_ones) + _dot(lo, bd_ones)


def _rwkv_kernel(p_ref, mu_ref, w0_ref, wb_ref, a0_ref, ab_ref, gb_ref, kkw_ref, ka_ref,
                 rk_ref, lng_ref, lnb_ref, o_ref,
                 carry_ref, state_ref, r_s, k_s, v_s, kk_s, b_s, lw_s, cum_s, y_s):
    ts = p_ref.shape[1]
    n_chunks = ts // RWKV_CHUNK
    n_groups = RWKV_WIDTH // GROUP_LANES

    @pl.when(pl.program_id(1) == 0)
    def _():
        carry_ref[...] = jnp.zeros_like(carry_ref)
        state_ref[...] = jnp.zeros_like(state_ref)

    row = lax.broadcasted_iota(jnp.int32, (ts, 1), 0)

    def shifted(c0, c1):
        x = p_ref[0, :, c0:c1]
        prev = pltpu.roll(x, 1, 0)
        prev = jnp.where(row == 0, carry_ref[:, c0:c1], prev)
        return x + (prev - x) * mu_ref[:, c0:c1]

    w1, w2, w3 = RWKV_WIDTH, 2 * RWKV_WIDTH, 3 * RWKV_WIDTH
    r = shifted(0, w1)
    k = shifted(w1, w2)
    v = shifted(w2, w3)
    wa_lo = shifted(w3, w3 + DECAY_LORA + ICL_LORA)
    g_lo = shifted(w3 + DECAY_LORA + ICL_LORA, RWKV_COLS)
    carry_ref[...] = p_ref[0, ts - 1:ts, :]

    li = lax.broadcasted_iota(jnp.int32, (RWKV_WIDTH, RWKV_WIDTH), 0) // RWKV_HEAD
    lj = lax.broadcasted_iota(jnp.int32, (RWKV_WIDTH, RWKV_WIDTH), 1) // RWKV_HEAD
    bd_ones = (li == lj).astype(BF16)

    z = w0_ref[...] + _dot(jnp.tanh(wa_lo).astype(BF16), wb_ref[...])
    nz = -z
    softplus = jnp.maximum(nz, 0.0) + jnp.log1p(jnp.exp(-jnp.abs(nz)))
    lw = -jnp.exp(-softplus - 0.5)
    a = jax.nn.sigmoid(a0_ref[...] + _dot(wa_lo.astype(BF16), ab_ref[...]))
    gate = _dot(jax.nn.sigmoid(g_lo).astype(BF16), gb_ref[...])
    kk = k * kkw_ref[...]
    kk = kk * lax.rsqrt(jnp.maximum(_head_sum(kk * kk, bd_ones), 1e-24))
    k2 = k * (1.0 + (a - 1.0) * ka_ref[...])
    r_s[...] = r
    k_s[...] = k2
    v_s[...] = v
    kk_s[...] = kk
    b_s[...] = kk * a
    lw_s[...] = lw

    ti = lax.broadcasted_iota(jnp.int32, (ts, ts), 0)
    tj = lax.broadcasted_iota(jnp.int32, (ts, ts), 1)
    tri = ((ti // RWKV_CHUNK == tj // RWKV_CHUNK) & (ti >= tj)).astype(BF16)
    l_hi, l_mid, l_lo = _split3(lw)
    cum_s[...] = _dot(tri, l_hi) + _dot(tri, l_mid) + _dot(tri, l_lo)

    gi = lax.broadcasted_iota(jnp.int32, (GROUP_LANES, GROUP_LANES), 0)
    gj = lax.broadcasted_iota(jnp.int32, (GROUP_LANES, GROUP_LANES), 1)
    same_head = (gi // RWKV_HEAD) == (gj // RWKV_HEAD)
    strict = same_head & (gi % RWKV_CHUNK > gj % RWKV_CHUNK)
    incl = same_head & (gi % RWKV_CHUNK >= gj % RWKV_CHUNK)
    eye = (gi == gj).astype(F32)

    def tile_heads(x):
        return jnp.concatenate([x] * RWKV_GROUP, axis=0)

    def fold_heads(x):
        out = x[0:RWKV_CHUNK]
        for i in range(1, RWKV_GROUP):
            out = out + x[i * RWKV_CHUNK:(i + 1) * RWKV_CHUNK]
        return out

    def chunk_body(c, carry):
        rows = pl.ds(pl.multiple_of(c * RWKV_CHUNK, RWKV_CHUNK), RWKV_CHUNK)
        for g in range(n_groups):
            lanes = slice(g * GROUP_LANES, (g + 1) * GROUP_LANES)
            rc, kc, vc = r_s[rows, lanes], k_s[rows, lanes], v_s[rows, lanes]
            kkc, bc, lwc, cum = kk_s[rows, lanes], b_s[rows, lanes], lw_s[rows, lanes], cum_s[rows, lanes]
            cum_end = cum[RWKV_CHUNK - 1:RWKV_CHUNK, :]
            e_neg = jnp.exp(-cum)
            e_end = jnp.exp(cum_end - cum)
            kkd = kkc * jnp.exp(cum - lwc)
            rd = rc * jnp.exp(cum)
            kinv = kc * e_neg
            binv = bc * e_neg
            kdp = kc * e_end
            bdp = bc * e_end

            kkd_s = jnp.where(same_head, tile_heads(kkd), 0.0).astype(BF16)
            rd_s = jnp.where(same_head, tile_heads(rd), 0.0).astype(BF16)
            v_st = jnp.where(same_head, tile_heads(vc), 0.0).astype(BF16)
            lhs2 = jnp.concatenate([kkd_s, rd_s], axis=0)
            g_b = _dot_nt(lhs2, tile_heads(binv).astype(BF16))
            g_k = _dot_nt(lhs2, tile_heads(kinv).astype(BF16))
            a_ab = jnp.where(strict, g_b[:GROUP_LANES], 0.0)
            a_rb = jnp.where(incl, g_b[GROUP_LANES:], 0.0).astype(BF16)
            a_ak = jnp.where(strict, g_k[:GROUP_LANES], 0.0).astype(BF16)
            a_rk = jnp.where(incl, g_k[GROUP_LANES:], 0.0).astype(BF16)
            av = _dot(jnp.concatenate([a_ak, a_rk], axis=0), v_st)
            z_v, y_loc = av[:GROUP_LANES], av[GROUP_LANES:]

            pw = a_ab.astype(BF16)
            t_inv = eye - a_ab
            for _ in range(int(math.log2(RWKV_CHUNK)) - 1):
                pw = _dot(pw, pw).astype(BF16)
                t_inv = t_inv + _dot(t_inv.astype(BF16), pw)
            x = _dot(t_inv.astype(BF16), jnp.concatenate([kkd_s, z_v.astype(BF16)], axis=1))
            tk_s, w0_s = x[:, :GROUP_LANES], x[:, GROUP_LANES:]

            state = state_ref[g]
            m1 = _dot_nt(jnp.concatenate([tk_s.astype(BF16), rd_s], axis=0), state.astype(BF16))
            sa_s = m1[:GROUP_LANES] + w0_s
            y_st = m1[GROUP_LANES:] + y_loc - _dot(a_rb, sa_s.astype(BF16))
            y_s[rows, lanes] = fold_heads(y_st)
            sa = fold_heads(sa_s)
            upd = _dot_tn(jnp.concatenate([vc, -sa], axis=0).astype(BF16),
                          jnp.concatenate([kdp, bdp], axis=0).astype(BF16))
            state_ref[g] = state * jnp.exp(cum_end) + jnp.where(same_head, upd, 0.0)
        return carry

    lax.fori_loop(0, n_chunks, chunk_body, 0)

    y = y_s[...]
    inv_n = 1.0 / RWKV_HEAD
    mean = _head_sum(y, bd_ones) * inv_n
    d = y - mean
    var = _head_sum(d * d, bd_ones) * inv_n
    yn = d * lax.rsqrt(var + GN_EPS) * lng_ref[...] + lnb_ref[...]
    bonus = _head_sum(r_s[...] * k_s[...] * rk_ref[...], bd_ones) * v_s[...]
    o_ref[0] = (yn + bonus) * gate


def _rwkv(proj3d, mu, w0, wb_pad, a0, ab_pad, gb, kkw, ka, rk, lng, lnb):
    bsz, seq, _ = proj3d.shape
    ts = ROW_TILE
    vec = lambda n: pl.BlockSpec((1, n), lambda b, s: (0, 0))
    mat = lambda m, n: pl.BlockSpec((m, n), lambda b, s: (0, 0))
    tile_f32 = pltpu.VMEM((ts, RWKV_WIDTH), F32)
    return pl.pallas_call(
        _rwkv_kernel,
        grid=(bsz, seq // ts),
        in_specs=[
            pl.BlockSpec((1, ts, RWKV_COLS), lambda b, s: (b, s, 0)),
            vec(RWKV_COLS), vec(RWKV_WIDTH), mat(DECAY_LORA + ICL_LORA, RWKV_WIDTH),
            vec(RWKV_WIDTH), mat(DECAY_LORA + ICL_LORA, RWKV_WIDTH), mat(GATE_LORA, RWKV_WIDTH),
            vec(RWKV_WIDTH), vec(RWKV_WIDTH), vec(RWKV_WIDTH), vec(RWKV_WIDTH), vec(RWKV_WIDTH),
        ],
        out_specs=pl.BlockSpec((1, ts, RWKV_WIDTH), lambda b, s: (b, s, 0)),
        out_shape=jax.ShapeDtypeStruct((bsz, seq, RWKV_WIDTH), F32),
        scratch_shapes=[
            pltpu.VMEM((1, RWKV_COLS), F32),
            pltpu.VMEM((RWKV_WIDTH // GROUP_LANES, GROUP_LANES, GROUP_LANES), F32),
            tile_f32, tile_f32, tile_f32, tile_f32, tile_f32, tile_f32, tile_f32, tile_f32,
        ],
        compiler_params=pltpu.CompilerParams(
            dimension_semantics=("arbitrary", "arbitrary"), vmem_limit_bytes=VMEM_LIMIT_BYTES),
        name="rwkv7",
    )(proj3d, mu, w0, wb_pad, a0, ab_pad, gb, kkw, ka, rk, lng, lnb)


def _rwkv_stage(proj3d, p):
    row = lambda t: t.reshape(1, -1)
    zeros = jnp.zeros((DECAY_LORA, RWKV_WIDTH), F32)
    wb_pad = jnp.concatenate([p['rwkv_wb'][0], zeros], axis=0).astype(BF16)
    ab_pad = jnp.concatenate([zeros, p['rwkv_ab'][0]], axis=0).astype(BF16)
    return _rwkv(proj3d, row(p['rwkv_mu'][0]), row(p['rwkv_w0'][0]), wb_pad, row(p['rwkv_a0'][0]), ab_pad,
                 p['rwkv_gb'][0].astype(BF16), row(p['rwkv_kk'][0]), row(p['rwkv_ka'][0]),
                 row(p['rwkv_rk'][0]), row(p['rwkv_ln_g'][0]), row(p['rwkv_ln_b'][0]))


ATT_TILE = 256
MASK_VALUE = -1e30


def _diff_kernel(slope_ref, lq1_ref, lk1_ref, lq2_ref, lk2_ref, g_ref, q_ref, k_ref, v_ref, o_ref):
    tq = q_ref.shape[1]
    qi = pl.program_id(2)
    slope = slope_ref[0]
    lane = lax.broadcasted_iota(jnp.int32, (1, 2 * DIFF_HEAD), 1)
    q = q_ref[0] * (DIFF_HEAD ** -0.5)
    q1 = jnp.where(lane < DIFF_HEAD, q, 0.0)
    q2 = jnp.where(lane >= DIFF_HEAD, q, 0.0)
    q12 = jnp.concatenate([q1, q2], axis=0).astype(BF16)
    rel = (lax.broadcasted_iota(jnp.int32, (tq, tq), 0)
           - lax.broadcasted_iota(jnp.int32, (tq, tq), 1)).astype(F32)
    rel = jnp.concatenate([rel, rel], axis=0)
    slope_t = slope[:, :1]

    def body(j, carry):
        m, l, acc = carry
        rows = pl.ds(pl.multiple_of(j * tq, tq), tq)
        kb = k_ref[0, rows, :].astype(BF16)
        vb = v_ref[0, rows, :].astype(BF16)
        dist = rel + ((qi - j) * tq).astype(F32)
        s = _dot_nt(q12, kb) - slope_t * dist
        s = jnp.where(dist >= 0.0, s, MASK_VALUE)
        m_new = jnp.maximum(m, jnp.max(s, axis=-1, keepdims=True))
        alpha = jnp.exp(m - m_new)
        pr = jnp.exp(s - m_new)
        l = alpha * l + jnp.sum(pr, axis=-1, keepdims=True)
        acc = alpha * acc + _dot(pr.astype(BF16), vb)
        return m_new, l, acc

    init = (jnp.full((2 * tq, 1), MASK_VALUE, F32), jnp.zeros((2 * tq, 1), F32),
            jnp.zeros((2 * tq, DIFF_VDIM), F32))
    m, l, acc = lax.fori_loop(0, qi + 1, body, init)
    o12 = acc / l
    lam = (jnp.exp(jnp.sum(lq1_ref[...] * lk1_ref[...], axis=-1, keepdims=True))
           - jnp.exp(jnp.sum(lq2_ref[...] * lk2_ref[...], axis=-1, keepdims=True)) + LAMBDA_INIT)
    o = o12[:tq] - lam * o12[tq:]
    o_ref[0] = _rms(o, g_ref[...]) * (1.0 - LAMBDA_INIT)


def _diff(proj3d, slopes, lq1, lk1, lq2, lk2, subln_g):
    bsz, seq, _ = proj3d.shape
    tq = ATT_TILE
    q_blk0 = RWKV_COLS // DIFF_VDIM
    k_blk0 = q_blk0 + DIFF_HEADS
    v_blk0 = k_blk0 + DIFF_HEADS
    vec = lambda n: pl.BlockSpec((1, n), lambda b, h, i: (0, 0))
    return pl.pallas_call(
        _diff_kernel,
        grid=(bsz, DIFF_HEADS, seq // tq),
        in_specs=[
            pl.BlockSpec((1, 1, LANES), lambda b, h, i: (h, 0, 0)),
            vec(DIFF_HEAD), vec(DIFF_HEAD), vec(DIFF_HEAD), vec(DIFF_HEAD), vec(DIFF_VDIM),
            pl.BlockSpec((1, tq, DIFF_VDIM), lambda b, h, i: (b, i, q_blk0 + h)),
            pl.BlockSpec((1, seq, DIFF_VDIM), lambda b, h, i: (b, 0, k_blk0 + h)),
            pl.BlockSpec((1, seq, DIFF_VDIM), lambda b, h, i: (b, 0, v_blk0 + h)),
        ],
        out_specs=pl.BlockSpec((1, tq, DIFF_VDIM), lambda b, h, i: (b, i, h)),
        out_shape=jax.ShapeDtypeStruct((bsz, seq, DIFF_WIDTH), F32),
        compiler_params=pltpu.CompilerParams(
            dimension_semantics=("arbitrary", "arbitrary", "arbitrary"),
            vmem_limit_bytes=VMEM_LIMIT_BYTES),
        name="diffattn",
    )(slopes, lq1, lk1, lq2, lk2, subln_g, proj3d, proj3d, proj3d)


def _diff_stage(proj3d, p):
    row = lambda t: t.reshape(1, -1)
    slopes = 2.0 ** (-8.0 * (jnp.arange(DIFF_HEADS, dtype=F32) + 1.0) / DIFF_HEADS)
    slopes = jnp.broadcast_to(slopes[:, None, None], (DIFF_HEADS, 1, LANES))
    return _diff(proj3d, slopes, row(p['diff_lq1'][0]), row(p['diff_lk1'][0]), row(p['diff_lq2'][0]),
                 row(p['diff_lk2'][0]), row(p['diff_subln_g'][0]))


PACK_COLS = D_MODEL // 2


def _pack_bf16_pairs(x):
    xb = x.astype(BF16).astype(F32)
    lo = lax.bitcast_convert_type(xb[:, :PACK_COLS], jnp.uint32) >> 16
    hi = lax.bitcast_convert_type(xb[:, PACK_COLS:], jnp.uint32)
    return lo | hi


def _unpack_bf16_pairs(xp):
    lo = lax.bitcast_convert_type(xp << 16, F32)
    hi = lax.bitcast_convert_type(xp & jnp.uint32(0xFFFF0000), F32)
    return jnp.concatenate([lo, hi], axis=1).astype(BF16)


def _mix_router_kernel(x_ref, yr_ref, yd_ref, wt_ref, wb_ref, g_ref, rwh_ref, rwl_ref, rb_ref,
                       h_ref, xp_ref, ti_ref, gt_ref, rk_ref, cnt_ref):
    tm = x_ref.shape[0]

    @pl.when(pl.program_id(0) == 0)
    def _():
        cnt_ref[...] = jnp.zeros_like(cnt_ref)

    h = (x_ref[...] + _dot(yr_ref[...].astype(BF16), wt_ref[...])
         + _dot(yd_ref[...].astype(BF16), wb_ref[...]))
    h_ref[...] = h
    xn = _rms(h, g_ref[...])
    xp_ref[...] = _pack_bf16_pairs(xn)
    x_hi, x_lo = _split2(xn)
    logits = (_dot(x_hi, rwh_ref[...]) + _dot(x_hi, rwl_ref[...]) + _dot(x_lo, rwh_ref[...])
              + rb_ref[...])

    lane_e = lax.broadcasted_iota(jnp.int32, (tm, N_EXPERTS), 1)
    lane_k = lax.broadcasted_iota(jnp.int32, (tm, TOP_K), 1)
    vals = logits
    tops, sels = [], []
    top_i = jnp.zeros((tm, TOP_K), jnp.int32)
    for k in range(TOP_K):
        m = jnp.max(vals, axis=-1, keepdims=True)
        idx = jnp.min(jnp.where(vals == m, lane_e, N_EXPERTS), axis=-1, keepdims=True)
        sel = lane_e == idx
        vals = jnp.where(sel, -jnp.inf, vals)
        tops.append(m)
        sels.append(sel)
        top_i = jnp.where(lane_k == k, idx, top_i)
    ti_ref[...] = top_i

    exps = [jnp.exp(m - tops[0]) for m in tops]
    den = exps[0] + exps[1] + exps[2] + exps[3]
    gates = jnp.zeros((tm, TOP_K), F32)
    for k in range(TOP_K):
        gates = jnp.where(lane_k == k, exps[k] / den, gates)
    gt_ref[...] = gates

    cnt = jnp.zeros((tm, N_EXPERTS), F32)
    for sel in sels:
        cnt = cnt + sel.astype(F32)
    ri = lax.broadcasted_iota(jnp.int32, (tm, tm), 0)
    rj = lax.broadcasted_iota(jnp.int32, (tm, tm), 1)
    before = (rj < ri).astype(BF16)
    prefix = _dot(before, cnt.astype(BF16)) + cnt_ref[...]
    rank = jnp.zeros((tm, TOP_K), F32)
    for k in range(TOP_K):
        rk = jnp.sum(jnp.where(sels[k], prefix, 0.0), axis=-1, keepdims=True)
        rank = jnp.where(lane_k == k, rk, rank)
    rk_ref[...] = rank.astype(jnp.int32)
    cnt_ref[...] += jnp.sum(cnt, axis=0, keepdims=True)


def _mix_router(x2d, yr2d, yd2d, w_top, w_bot, g2, rw_hi, rw_lo, rb):
    n_tok = x2d.shape[0]
    tm = ROW_TILE
    rows = lambda n: pl.BlockSpec((tm, n), lambda i: (i, 0))
    full = lambda m, n: pl.BlockSpec((m, n), lambda i: (0, 0))
    return pl.pallas_call(
        _mix_router_kernel,
        grid=(n_tok // tm,),
        in_specs=[rows(D_MODEL), rows(RWKV_WIDTH), rows(DIFF_WIDTH),
                  full(RWKV_WIDTH, D_MODEL), full(DIFF_WIDTH, D_MODEL), full(1, D_MODEL),
                  full(D_MODEL, N_EXPERTS), full(D_MODEL, N_EXPERTS), full(1, N_EXPERTS)],
        out_specs=[rows(D_MODEL), rows(PACK_COLS), rows(TOP_K), rows(TOP_K), rows(TOP_K),
                   full(1, N_EXPERTS)],
        out_shape=[jax.ShapeDtypeStruct((n_tok, D_MODEL), F32),
                   jax.ShapeDtypeStruct((n_tok, PACK_COLS), jnp.uint32),
                   jax.ShapeDtypeStruct((n_tok, TOP_K), jnp.int32),
                   jax.ShapeDtypeStruct((n_tok, TOP_K), F32),
                   jax.ShapeDtypeStruct((n_tok, TOP_K), jnp.int32),
                   jax.ShapeDtypeStruct((1, N_EXPERTS), F32)],
        compiler_params=pltpu.CompilerParams(
            dimension_semantics=("arbitrary",), vmem_limit_bytes=VMEM_LIMIT_BYTES),
        name="mix_router",
    )(x2d, yr2d, yd2d, w_top, w_bot, g2, rw_hi, rw_lo, rb)


EXPERT_TILE = 512


def _dispatch_kernel(dest_ref, xp_ref, init_ref, xs_ref, sem):
    del init_ref
    tm = xp_ref.shape[0]

    def issue(t, carry):
        for k in range(TOP_K):
            d = dest_ref[0, 0, t * TOP_K + k]
            pltpu.make_async_copy(xp_ref.at[pl.ds(t, 1), :], xs_ref.at[pl.ds(d, 1), :], sem).start()
        return carry

    lax.fori_loop(0, tm, issue, 0, unroll=8)
    for k in range(TOP_K):
        pltpu.make_async_copy(xp_ref, xs_ref.at[pl.ds(0, tm), :], sem).wait()


def _dispatch(dest3d, xp, xs_init):
    n_tok = xp.shape[0]
    tm = ROW_TILE
    return pl.pallas_call(
        _dispatch_kernel,
        grid=(n_tok // tm,),
        in_specs=[pl.BlockSpec((1, 1, tm * TOP_K), lambda i: (i, 0, 0), memory_space=pltpu.SMEM),
                  pl.BlockSpec((tm, PACK_COLS), lambda i: (i, 0)),
                  pl.BlockSpec(memory_space=pl.ANY)],
        out_specs=pl.BlockSpec(memory_space=pl.ANY),
        out_shape=jax.ShapeDtypeStruct(xs_init.shape, xs_init.dtype),
        scratch_shapes=[pltpu.SemaphoreType.DMA(())],
        input_output_aliases={2: 0},
        compiler_params=pltpu.CompilerParams(
            dimension_semantics=("arbitrary",), vmem_limit_bytes=VMEM_LIMIT_BYTES),
        name="dispatch",
    )(dest3d, xp, xs_init)


def _expert_kernel(be_ref, src_ref, nu_ref, xs_ref, wg_ref, wl_ref, wd_ref, bg_ref, bl_ref, bd_ref, y_ref):
    del be_ref, src_ref
    used = pl.program_id(0) < nu_ref[0]

    @pl.when(jnp.logical_not(used))
    def _():
        y_ref[...] = jnp.zeros_like(y_ref)

    @pl.when(used)
    def _():
        x = _unpack_bf16_pairs(xs_ref[...])
        glu = jnp.minimum(_dot(x, wg_ref[0]) + bg_ref[0], SWIGLU_LIMIT)
        lin = jnp.clip(_dot(x, wl_ref[0]) + bl_ref[0], -SWIGLU_LIMIT, SWIGLU_LIMIT)
        act = glu * jax.nn.sigmoid(SWIGLU_ALPHA * glu) * (lin + 1.0)
        y_ref[...] = _dot(act.astype(BF16), wd_ref[0]) + bd_ref[0]


def _experts(blk_expert, blk_src, n_used, xs, wg, wl, wd, bg, bl, bd):
    n_slots = xs.shape[0]
    tb = EXPERT_TILE
    wspec = lambda k, n: pl.BlockSpec((1, k, n), lambda j, be, src, nu: (be[j], 0, 0))
    return pl.pallas_call(
        _expert_kernel,
        grid_spec=pltpu.PrefetchScalarGridSpec(
            num_scalar_prefetch=3,
            grid=(n_slots // tb,),
            in_specs=[pl.BlockSpec((tb, PACK_COLS), lambda j, be, src, nu: (src[j], 0)),
                      wspec(D_MODEL, D_FF), wspec(D_MODEL, D_FF), wspec(D_FF, D_MODEL),
                      wspec(1, D_FF), wspec(1, D_FF), wspec(1, D_MODEL)],
            out_specs=pl.BlockSpec((tb, D_MODEL), lambda j, be, src, nu: (j, 0)),
        ),
        out_shape=jax.ShapeDtypeStruct((n_slots, D_MODEL), F32),
        compiler_params=pltpu.CompilerParams(
            dimension_semantics=("arbitrary",), vmem_limit_bytes=VMEM_LIMIT_BYTES),
        name="experts",
    )(blk_expert, blk_src, n_used, xs, wg, wl, wd, bg, bl, bd)


COMBINE_TILE = 256


def _combine_kernel(dest_ref, h_ref, gt_ref, g_ref, ys_ref, o_ref, buf, sem):
    tm = h_ref.shape[0]

    def issue(t, carry):
        for k in range(TOP_K):
            d = dest_ref[0, 0, t * TOP_K + k]
            pltpu.make_async_copy(ys_ref.at[pl.ds(d, 1), :], buf.at[k, pl.ds(t, 1), :], sem).start()
        return carry

    lax.fori_loop(0, tm, issue, 0, unroll=8)
    for k in range(TOP_K):
        pltpu.make_async_copy(ys_ref.at[pl.ds(0, tm), :], buf.at[k], sem).wait()
    gates = gt_ref[...]
    out = h_ref[...]
    for k in range(TOP_K):
        out = out + gates[:, k:k + 1] * buf[k]
    o_ref[...] = _rms(out, g_ref[...])


def _combine(dest3d, h, gates, final_g, ys):
    n_tok = h.shape[0]
    tm = COMBINE_TILE
    return pl.pallas_call(
        _combine_kernel,
        grid=(n_tok // tm,),
        in_specs=[pl.BlockSpec((1, 1, tm * TOP_K), lambda i: (i, 0, 0), memory_space=pltpu.SMEM),
                  pl.BlockSpec((tm, D_MODEL), lambda i: (i, 0)),
                  pl.BlockSpec((tm, TOP_K), lambda i: (i, 0)),
                  pl.BlockSpec((1, D_MODEL), lambda i: (0, 0)),
                  pl.BlockSpec(memory_space=pl.ANY)],
        out_specs=pl.BlockSpec((tm, D_MODEL), lambda i: (i, 0)),
        out_shape=jax.ShapeDtypeStruct((n_tok, D_MODEL), F32),
        scratch_shapes=[pltpu.VMEM((TOP_K, tm, D_MODEL), F32), pltpu.SemaphoreType.DMA(())],
        compiler_params=pltpu.CompilerParams(
            dimension_semantics=("arbitrary",), vmem_limit_bytes=VMEM_LIMIT_BYTES),
        name="combine",
    )(dest3d, h, gates, final_g, ys)


def _moe_stage(x2d, yr2d, yd2d, p):
    n_tok = x2d.shape[0]
    row = lambda t: t.reshape(1, -1)
    w_out = p['w_out'][0].astype(BF16)
    rw = p['router_w'][0]
    rw_hi = rw.astype(BF16)
    rw_lo = (rw - rw_hi.astype(F32)).astype(BF16)
    h, xp, top_i, gates, rank, counts = _mix_router(
        x2d, yr2d, yd2d, w_out[:RWKV_WIDTH], w_out[RWKV_WIDTH:], row(p['norm2_g'][0]),
        rw_hi, rw_lo, row(p['router_b'][0]))

    tb = EXPERT_TILE
    n_blocks = (n_tok * TOP_K) // tb + N_EXPERTS
    counts = counts[0].astype(jnp.int32)
    padded = (counts + tb - 1) // tb * tb
    pad_end = jnp.cumsum(padded)
    pad_start = pad_end - padded
    dest = pad_start[top_i] + rank
    n_used = pad_end[-1] // tb
    blk_src = jnp.minimum(jnp.arange(n_blocks, dtype=jnp.int32), n_used - 1)
    blk_expert = jnp.minimum(
        jnp.searchsorted(pad_end, blk_src * tb, side='right'), N_EXPERTS - 1).astype(jnp.int32)

    xs_init = jnp.zeros((n_blocks * tb, PACK_COLS), jnp.uint32)
    xs = _dispatch(dest.reshape(n_tok // ROW_TILE, 1, ROW_TILE * TOP_K), xp, xs_init)

    w_up = p['exp_w_up'][0]
    b_up = p['exp_b_up'][0]
    wg, wl = w_up[:, :, 0::2].astype(BF16), w_up[:, :, 1::2].astype(BF16)
    bg, bl = b_up[:, None, 0::2], b_up[:, None, 1::2]
    ys = _experts(blk_expert, blk_src.astype(jnp.int32), n_used.reshape(1).astype(jnp.int32), xs, wg, wl,
                  p['exp_w_down'][0].astype(BF16), bg, bl, p['exp_b_down'][0][:, None, :])
    return _combine(dest.reshape(n_tok // COMBINE_TILE, 1, COMBINE_TILE * TOP_K), h, gates,
                    row(p['final_g']), ys)


def _moe_stage_test(h2d, p):
    zeros = jnp.zeros((h2d.shape[0], RWKV_WIDTH), F32)
    return _moe_stage(h2d, zeros, zeros, p)


def kernel(x, norm1_g, w_in, rwkv_mu, rwkv_w0, rwkv_wb, rwkv_a0, rwkv_ab, rwkv_gb, rwkv_kk, rwkv_ka, rwkv_rk, rwkv_ln_g, rwkv_ln_b, diff_lq1, diff_lk1, diff_lq2, diff_lk2, diff_subln_g, w_out, norm2_g, router_w, router_b, exp_w_up, exp_b_up, exp_w_down, exp_b_down, final_g):
    p = dict(rwkv_mu=rwkv_mu, rwkv_w0=rwkv_w0, rwkv_wb=rwkv_wb, rwkv_a0=rwkv_a0, rwkv_ab=rwkv_ab,
             rwkv_gb=rwkv_gb, rwkv_kk=rwkv_kk, rwkv_ka=rwkv_ka, rwkv_rk=rwkv_rk,
             rwkv_ln_g=rwkv_ln_g, rwkv_ln_b=rwkv_ln_b, diff_lq1=diff_lq1, diff_lk1=diff_lk1,
             diff_lq2=diff_lq2, diff_lk2=diff_lk2, diff_subln_g=diff_subln_g, w_out=w_out,
             norm2_g=norm2_g, router_w=router_w, router_b=router_b, exp_w_up=exp_w_up,
             exp_b_up=exp_b_up, exp_w_down=exp_w_down, exp_b_down=exp_b_down, final_g=final_g)
    bsz, seq, d = x.shape
    n_tok = bsz * seq
    x2d = x.reshape(n_tok, d)
    proj = _inproj(x2d, norm1_g, w_in[0].astype(BF16)).reshape(bsz, seq, IN_COLS)
    y_rwkv = _rwkv_stage(proj, p).reshape(n_tok, RWKV_WIDTH)
    y_diff = _diff_stage(proj, p).reshape(n_tok, DIFF_WIDTH)
    return _moe_stage(x2d, y_rwkv, y_diff, p).reshape(bsz, seq, d)
```

```python
import functools
import math

import jax
import jax.numpy as jnp
from jax import lax
from jax.experimental import pallas as pl
from jax.experimental.pallas import tpu as pltpu

F32 = jnp.float32
BF16 = jnp.bfloat16

D_MODEL = 1024
RWKV_WIDTH = 512
RWKV_HEAD = 64
DECAY_LORA = 64
ICL_LORA = 64
GATE_LORA = 128
GN_EPS = 64e-5
DIFF_WIDTH = 512
DIFF_HEAD = 64
DIFF_HEADS = 4
DIFF_VDIM = 128
RWKV_COLS = 3 * RWKV_WIDTH + DECAY_LORA + ICL_LORA + GATE_LORA
IN_COLS = RWKV_COLS + 3 * DIFF_WIDTH
N_EXPERTS = 32
TOP_K = 4
D_FF = 1024
SWIGLU_ALPHA = 1.702
SWIGLU_LIMIT = 7.0
NORM_EPS = 1e-5
LAMBDA_INIT = 0.8 - 0.6 * math.exp(-0.0)

LANES = 128
VMEM_LIMIT_BYTES = 56 * 1024 * 1024

ROW_TILE = 512
RWKV_CHUNK = 64
RWKV_GROUP = 4
GROUP_LANES = RWKV_GROUP * RWKV_HEAD
PREP_UNROLL = 2


def _dot(a, b):
    return jnp.dot(a, b, preferred_element_type=F32)


def _dot_nt(a, b):
    return lax.dot_general(a, b, (((1,), (1,)), ((), ())), preferred_element_type=F32)


def _dot_tn(a, b):
    return lax.dot_general(a, b, (((0,), (0,)), ((), ())), preferred_element_type=F32)


def _split2(x):
    hi = x.astype(BF16)
    lo = (x - hi.astype(F32)).astype(BF16)
    return hi, lo


def _split3(x):
    hi = x.astype(BF16)
    r1 = x - hi.astype(F32)
    mid = r1.astype(BF16)
    lo = (r1 - mid.astype(F32)).astype(BF16)
    return hi, mid, lo


def _rms(x, g):
    return x * lax.rsqrt(jnp.mean(x * x, axis=-1, keepdims=True) + NORM_EPS) * g


def _inproj_kernel(x_ref, g_ref, w_ref, o_ref):
    xn = _rms(x_ref[...], g_ref[...])
    o_ref[...] = _dot(xn.astype(BF16), w_ref[...])


def _inproj(x2d, g, w_bf16):
    n_tok = x2d.shape[0]
    return pl.pallas_call(
        _inproj_kernel,
        grid=(n_tok // ROW_TILE,),
        in_specs=[
            pl.BlockSpec((ROW_TILE, D_MODEL), lambda i: (i, 0)),
            pl.BlockSpec((1, D_MODEL), lambda i: (0, 0)),
            pl.BlockSpec((D_MODEL, IN_COLS), lambda i: (0, 0)),
        ],
        out_specs=pl.BlockSpec((ROW_TILE, IN_COLS), lambda i: (i, 0)),
        out_shape=jax.ShapeDtypeStruct((n_tok, IN_COLS), F32),
        compiler_params=pltpu.CompilerParams(
            dimension_semantics=("arbitrary",), vmem_limit_bytes=VMEM_LIMIT_BYTES),
        name="inproj",
    )(x2d, g, w_bf16)


def _head_sum(x, bd_ones):
    xb = x.astype(BF16)
    return jnp.concatenate(
        [_dot(xb[:, g * GROUP_LANES:(g + 1) * GROUP_LANES], bd_ones)
         for g in range(RWKV_WIDTH // GROUP_LANES)], axis=1)


def _rwkv_kernel(p_ref, mu_ref, w0_ref, wb_ref, a0_ref, ab_ref, gb_ref, kkw_ref, ka_ref,
                 rk_ref, lng_ref, lnb_ref, o_ref,
                 carry_ref, state_ref, r_s, k_s, v_s, kk_s, b_s, lw_s, y_s,
                 lhs_s, w0_s, yloc_s, arb_s, kb_s, dec_s):
    ts = p_ref.shape[1]
    n_chunks = ts // RWKV_CHUNK
    n_groups = RWKV_WIDTH // GROUP_LANES

    @pl.when(pl.program_id(1) == 0)
    def _():
        carry_ref[...] = jnp.zeros_like(carry_ref)
        state_ref[...] = jnp.zeros_like(state_ref)

    row = lax.broadcasted_iota(jnp.int32, (ts, 1), 0)

    def shifted(c0, c1):
        x = p_ref[0, :, c0:c1]
        prev = pltpu.roll(x, 1, 0)
        prev = jnp.where(row == 0, carry_ref[:, c0:c1], prev)
        return x + (prev - x) * mu_ref[:, c0:c1]

    w1, w2, w3 = RWKV_WIDTH, 2 * RWKV_WIDTH, 3 * RWKV_WIDTH
    r = shifted(0, w1)
    k = shifted(w1, w2)
    v = shifted(w2, w3)
    wa_lo = shifted(w3, w3 + DECAY_LORA + ICL_LORA)
    g_lo = shifted(w3 + DECAY_LORA + ICL_LORA, RWKV_COLS)
    carry_ref[...] = p_ref[0, ts - 1:ts, :]

    gi = lax.broadcasted_iota(jnp.int32, (GROUP_LANES, GROUP_LANES), 0)
    gj = lax.broadcasted_iota(jnp.int32, (GROUP_LANES, GROUP_LANES), 1)
    same_head = (gi // RWKV_HEAD) == (gj // RWKV_HEAD)
    strict = same_head & (gi % RWKV_CHUNK > gj % RWKV_CHUNK)
    incl = same_head & (gi % RWKV_CHUNK >= gj % RWKV_CHUNK)
    eye = (gi == gj).astype(F32)
    bd_ones = same_head.astype(BF16)

    z = w0_ref[...] + _dot(jnp.tanh(wa_lo).astype(BF16), wb_ref[...])
    nz = -z
    softplus = jnp.maximum(nz, 0.0) + jnp.log1p(jnp.exp(-jnp.abs(nz)))
    lw = -jnp.exp(-softplus - 0.5)
    a = jax.nn.sigmoid(a0_ref[...] + _dot(wa_lo.astype(BF16), ab_ref[...]))
    gate = _dot(jax.nn.sigmoid(g_lo).astype(BF16), gb_ref[...])
    kk = k * kkw_ref[...]
    kk = kk * lax.rsqrt(jnp.maximum(_head_sum(kk * kk, bd_ones), 1e-24))
    k2 = k * (1.0 + (a - 1.0) * ka_ref[...])
    r_s[...] = r
    k_s[...] = k2
    v_s[...] = v
    kk_s[...] = kk
    b_s[...] = kk * a
    lw_s[...] = lw

    ci = lax.broadcasted_iota(jnp.int32, (RWKV_CHUNK, RWKV_CHUNK), 0)
    cj = lax.broadcasted_iota(jnp.int32, (RWKV_CHUNK, RWKV_CHUNK), 1)
    tri = (ci >= cj).astype(BF16)
    tri2 = jnp.concatenate([tri, tri], axis=1)

    def tile_heads(x):
        return jnp.concatenate([x] * RWKV_GROUP, axis=0)

    def fold_heads(x):
        out = x[0:RWKV_CHUNK]
        for i in range(1, RWKV_GROUP):
            out = out + x[i * RWKV_CHUNK:(i + 1) * RWKV_CHUNK]
        return out

    def chunk_rows(c):
        return pl.ds(pl.multiple_of(c * RWKV_CHUNK, RWKV_CHUNK), RWKV_CHUNK)

    def prepare(cp, carry):
        units = [(cp * PREP_UNROLL + i, g) for i in range(PREP_UNROLL) for g in range(n_groups)]
        lanes = [slice(g * GROUP_LANES, (g + 1) * GROUP_LANES) for _, g in units]
        rows = [chunk_rows(c) for c, _ in units]
        load = lambda ref: [ref[rw, ln] for rw, ln in zip(rows, lanes)]
        rc, kc, vc, kkc, bc, lwc = (load(ref) for ref in (r_s, k_s, v_s, kk_s, b_s, lw_s))
        cum = [_dot(tri2, jnp.concatenate(_split2(x), axis=0)) for x in lwc]
        cum_end = [x[RWKV_CHUNK - 1:RWKV_CHUNK, :] for x in cum]
        for (c, _), ln, ce in zip(units, lanes, cum_end):
            dec_s[c, :, ln] = jnp.exp(ce)
        e_neg = [jnp.exp(-x) for x in cum]
        e_end = [jnp.exp(ce - x) for ce, x in zip(cum_end, cum)]
        kkd = [a * jnp.exp(x - lw) for a, x, lw in zip(kkc, cum, lwc)]
        rd = [a * jnp.exp(x) for a, x in zip(rc, cum)]
        spread = lambda xs: [jnp.where(same_head, tile_heads(x), 0.0).astype(BF16) for x in xs]
        kkd_s, rd_s, v_st = spread(kkd), spread(rd), spread(vc)
        rhs = [jnp.concatenate([tile_heads(b * e), tile_heads(k * e)], axis=0).astype(BF16)
               for b, k, e in zip(bc, kc, e_neg)]
        gram = [_dot_nt(jnp.concatenate([a, b], axis=0), w) for a, b, w in zip(kkd_s, rd_s, rhs)]
        gl = GROUP_LANES
        a_ab = [jnp.where(strict, x[:gl, :gl], 0.0) for x in gram]
        a_rb = [jnp.where(incl, x[gl:, :gl], 0.0).astype(BF16) for x in gram]
        a_ak = [jnp.where(strict, x[:gl, gl:], 0.0).astype(BF16) for x in gram]
        a_rk = [jnp.where(incl, x[gl:, gl:], 0.0).astype(BF16) for x in gram]
        av = [_dot(jnp.concatenate([a, b], axis=0), v) for a, b, v in zip(a_ak, a_rk, v_st)]

        pw = [x.astype(BF16) for x in a_ab]
        t_inv = [eye - x for x in a_ab]
        for _ in range(int(math.log2(RWKV_CHUNK)) - 1):
            pw = [_dot(x, x).astype(BF16) for x in pw]
            t_inv = [t + _dot(t.astype(BF16), x) for t, x in zip(t_inv, pw)]
        sol = [_dot(t.astype(BF16), jnp.concatenate([a, x[:gl].astype(BF16)], axis=1))
               for t, a, x in zip(t_inv, kkd_s, av)]
        for i, (c, g) in enumerate(units):
            lhs_s[c, g] = jnp.concatenate([sol[i][:, :gl].astype(BF16), rd_s[i]], axis=0)
            w0_s[c, g] = sol[i][:, gl:]
            yloc_s[c, g] = fold_heads(av[i][gl:])
            arb_s[c, g] = a_rb[i]
            kb_s[c, g] = jnp.concatenate([kc[i] * e_end[i], bc[i] * e_end[i]], axis=0).astype(BF16)
        return carry

    def advance(c, carry):
        rows = chunk_rows(c)
        gs = range(n_groups)
        lanes = [slice(g * GROUP_LANES, (g + 1) * GROUP_LANES) for g in gs]
        state = [state_ref[g] for g in gs]
        m1 = [_dot_nt(lhs_s[c, g], state[g].astype(BF16)) for g in gs]
        sa_st = [m1[g][:GROUP_LANES] + w0_s[c, g] for g in gs]
        y_st = [m1[g][GROUP_LANES:] - _dot(arb_s[c, g], sa_st[g].astype(BF16)) for g in gs]
        upd = [_dot_tn(jnp.concatenate([v_s[rows, lanes[g]], -fold_heads(sa_st[g])], axis=0).astype(BF16),
                       kb_s[c, g]) for g in gs]
        for g in gs:
            y_s[rows, lanes[g]] = fold_heads(y_st[g]) + yloc_s[c, g]
            state_ref[g] = state[g] * dec_s[c, :, lanes[g]] + jnp.where(same_head, upd[g], 0.0)
        return carry

    lax.fori_loop(0, n_chunks // PREP_UNROLL, prepare, 0)
    lax.fori_loop(0, n_chunks, advance, 0)

    y = y_s[...]
    inv_n = 1.0 / RWKV_HEAD
    mean = _head_sum(y, bd_ones) * inv_n
    d = y - mean
    var = _head_sum(d * d, bd_ones) * inv_n
    yn = d * lax.rsqrt(var + GN_EPS) * lng_ref[...] + lnb_ref[...]
    bonus = _head_sum(r_s[...] * k_s[...] * rk_ref[...], bd_ones) * v_s[...]
    o_ref[0] = (yn + bonus) * gate


def _rwkv(proj3d, mu, w0, wb_pad, a0, ab_pad, gb, kkw, ka, rk, lng, lnb):
    bsz, seq, _ = proj3d.shape
    ts = ROW_TILE
    vec = lambda n: pl.BlockSpec((1, n), lambda b, s: (0, 0))
    mat = lambda m, n: pl.BlockSpec((m, n), lambda b, s: (0, 0))
    tile_f32 = pltpu.VMEM((ts, RWKV_WIDTH), F32)
    n_chunks = ts // RWKV_CHUNK
    n_groups = RWKV_WIDTH // GROUP_LANES
    return pl.pallas_call(
        _rwkv_kernel,
        grid=(bsz, seq // ts),
        in_specs=[
            pl.BlockSpec((1, ts, RWKV_COLS), lambda b, s: (b, s, 0)),
            vec(RWKV_COLS), vec(RWKV_WIDTH), mat(DECAY_LORA + ICL_LORA, RWKV_WIDTH),
            vec(RWKV_WIDTH), mat(DECAY_LORA + ICL_LORA, RWKV_WIDTH), mat(GATE_LORA, RWKV_WIDTH),
            vec(RWKV_WIDTH), vec(RWKV_WIDTH), vec(RWKV_WIDTH), vec(RWKV_WIDTH), vec(RWKV_WIDTH),
        ],
        out_specs=pl.BlockSpec((1, ts, RWKV_WIDTH), lambda b, s: (b, s, 0)),
        out_shape=jax.ShapeDtypeStruct((bsz, seq, RWKV_WIDTH), F32),
        scratch_shapes=[
            pltpu.VMEM((1, RWKV_COLS), F32),
            pltpu.VMEM((n_groups, GROUP_LANES, GROUP_LANES), F32),
            tile_f32, tile_f32, tile_f32, tile_f32, tile_f32, tile_f32, tile_f32,
            pltpu.VMEM((n_chunks, n_groups, 2 * GROUP_LANES, GROUP_LANES), BF16),
            pltpu.VMEM((n_chunks, n_groups, GROUP_LANES, GROUP_LANES), F32),
            pltpu.VMEM((n_chunks, n_groups, RWKV_CHUNK, GROUP_LANES), F32),
            pltpu.VMEM((n_chunks, n_groups, GROUP_LANES, GROUP_LANES), BF16),
            pltpu.VMEM((n_chunks, n_groups, 2 * RWKV_CHUNK, GROUP_LANES), BF16),
            pltpu.VMEM((n_chunks, 1, RWKV_WIDTH), F32),
        ],
        compiler_params=pltpu.CompilerParams(
            dimension_semantics=("arbitrary", "arbitrary"), vmem_limit_bytes=VMEM_LIMIT_BYTES),
        name="rwkv7",
    )(proj3d, mu, w0, wb_pad, a0, ab_pad, gb, kkw, ka, rk, lng, lnb)


def _rwkv_stage(proj3d, p):
    row = lambda t: t.reshape(1, -1)
    zeros = jnp.zeros((DECAY_LORA, RWKV_WIDTH), F32)
    wb_pad = jnp.concatenate([p['rwkv_wb'][0], zeros], axis=0).astype(BF16)
    ab_pad = jnp.concatenate([zeros, p['rwkv_ab'][0]], axis=0).astype(BF16)
    return _rwkv(proj3d, row(p['rwkv_mu'][0]), row(p['rwkv_w0'][0]), wb_pad, row(p['rwkv_a0'][0]), ab_pad,
                 p['rwkv_gb'][0].astype(BF16), row(p['rwkv_kk'][0]), row(p['rwkv_ka'][0]),
                 row(p['rwkv_rk'][0]), row(p['rwkv_ln_g'][0]), row(p['rwkv_ln_b'][0]))


ATT_TILE = 256
ATT_LOCKSTEP = 2
ATT_AUG = 2 * LANES
POS_SPLIT_BITS = 6
MASK_VALUE = -1e30


def _alibi_lanes(n, pos0, slope, key_side):
    pos = pos0 + lax.broadcasted_iota(jnp.int32, (n, LANES), 0)
    lane = lax.broadcasted_iota(jnp.int32, (n, LANES), 1)
    hi = (pos >> POS_SPLIT_BITS).astype(F32) * (slope * float(1 << POS_SPLIT_BITS))
    lo = (pos & ((1 << POS_SPLIT_BITS) - 1)).astype(F32) * slope
    if key_side:
        return jnp.where(lane == 0, hi, jnp.where(lane == 1, lo, jnp.where(lane < 4, 1.0, 0.0)))
    return jnp.where(lane < 2, 1.0, jnp.where(lane == 2, -hi, jnp.where(lane == 3, -lo, 0.0)))


def _diff_kernel(slope_ref, lq1_ref, lk1_ref, lq2_ref, lk2_ref, g_ref, *refs):
    nh = DIFF_HEADS
    q_refs, k_refs, v_refs = refs[:nh], refs[nh:2 * nh], refs[2 * nh:3 * nh]
    o_ref, ks_ref, vs_ref, m_ref, acc_ref = refs[3 * nh:]
    tq = q_refs[0].shape[1]
    seq = k_refs[0].shape[1]
    qi = pl.program_id(1)
    slopes = [slope_ref[h][:, :1] for h in range(nh)]

    @pl.when(qi == 0)
    def _():
        for h in range(nh):
            ks_ref[h, :, :LANES] = k_refs[h][0].astype(BF16)
            ks_ref[h, :, LANES:] = _alibi_lanes(seq, 0, slopes[h], True).astype(BF16)
            vs_ref[h, :, :LANES] = v_refs[h][0].astype(BF16)
            vs_ref[h, :, LANES:] = jnp.ones((seq, LANES), BF16)

    lane = lax.broadcasted_iota(jnp.int32, (1, LANES), 1)
    qa = []
    for h in range(nh):
        q = q_refs[h][0] * (DIFF_HEAD ** -0.5)
        q_terms = _alibi_lanes(tq, qi * tq, slopes[h], False)
        qa.append(jnp.concatenate(
            [jnp.concatenate([jnp.where(lane < DIFF_HEAD, q, 0.0), q_terms], axis=1),
             jnp.concatenate([jnp.where(lane >= DIFF_HEAD, q, 0.0), q_terms], axis=1)],
            axis=0).astype(BF16))
    m_ref[...] = jnp.full_like(m_ref, MASK_VALUE)
    acc_ref[...] = jnp.zeros_like(acc_ref)
    causal = (lax.broadcasted_iota(jnp.int32, (tq, tq), 0)
              >= lax.broadcasted_iota(jnp.int32, (tq, tq), 1))
    causal = jnp.concatenate([causal, causal], axis=0)

    def process(j, diagonal):
        rows = pl.ds(pl.multiple_of(j * tq, tq), tq)
        for h0 in range(0, nh, ATT_LOCKSTEP):
            hs = range(h0, h0 + ATT_LOCKSTEP)
            s = {h: _dot_nt(qa[h], ks_ref[h, rows, :]) for h in hs}
            if diagonal:
                s = {h: jnp.where(causal, s[h], MASK_VALUE) for h in hs}
            m_old = {h: m_ref[h] for h in hs}
            m_new = {h: jnp.maximum(m_old[h], jnp.max(s[h], axis=-1, keepdims=True)) for h in hs}
            pr = {h: jnp.exp(s[h] - jnp.concatenate([m_new[h], m_new[h]], axis=1)).astype(BF16)
                  for h in hs}
            pv = {h: _dot(pr[h], vs_ref[h, rows, :]) for h in hs}
            for h in hs:
                alpha = jnp.exp(m_old[h] - m_new[h])
                acc_ref[h] = jnp.concatenate([alpha, alpha], axis=1) * acc_ref[h] + pv[h]
                m_ref[h] = m_new[h]

    def off_diagonal(j, carry):
        process(j, False)
        return carry

    lax.fori_loop(0, qi, off_diagonal, 0)
    process(qi, True)

    lam = (jnp.exp(jnp.sum(lq1_ref[...] * lk1_ref[...], axis=-1, keepdims=True))
           - jnp.exp(jnp.sum(lq2_ref[...] * lk2_ref[...], axis=-1, keepdims=True)) + LAMBDA_INIT)
    for h in range(nh):
        acc = acc_ref[h]
        o12 = acc[:, :DIFF_VDIM] / acc[:, DIFF_VDIM:]
        o = o12[:tq] - lam * o12[tq:]
        o_ref[0, :, h * DIFF_VDIM:(h + 1) * DIFF_VDIM] = _rms(o, g_ref[...]) * (1.0 - LAMBDA_INIT)


def _diff(proj3d, slopes, lq1, lk1, lq2, lk2, subln_g):
    bsz, seq, _ = proj3d.shape
    tq = ATT_TILE
    q_blk0 = RWKV_COLS // DIFF_VDIM
    k_blk0 = q_blk0 + DIFF_HEADS
    v_blk0 = k_blk0 + DIFF_HEADS
    nh = DIFF_HEADS
    vec = lambda n: pl.BlockSpec((1, n), lambda b, i: (0, 0))
    q_spec = lambda h: pl.BlockSpec((1, tq, DIFF_VDIM), lambda b, i: (b, i, q_blk0 + h))
    kv_spec = lambda blk: pl.BlockSpec((1, seq, DIFF_VDIM), lambda b, i: (b, 0, blk))
    return pl.pallas_call(
        _diff_kernel,
        grid=(bsz, seq // tq),
        in_specs=([pl.BlockSpec((nh, 1, LANES), lambda b, i: (0, 0, 0)),
                   vec(DIFF_HEAD), vec(DIFF_HEAD), vec(DIFF_HEAD), vec(DIFF_HEAD), vec(DIFF_VDIM)]
                  + [q_spec(h) for h in range(nh)]
                  + [kv_spec(k_blk0 + h) for h in range(nh)]
                  + [kv_spec(v_blk0 + h) for h in range(nh)]),
        out_specs=pl.BlockSpec((1, tq, DIFF_WIDTH), lambda b, i: (b, i, 0)),
        out_shape=jax.ShapeDtypeStruct((bsz, seq, DIFF_WIDTH), F32),
        scratch_shapes=[pltpu.VMEM((nh, seq, ATT_AUG), BF16), pltpu.VMEM((nh, seq, ATT_AUG), BF16),
                        pltpu.VMEM((nh, 2 * tq, LANES), F32), pltpu.VMEM((nh, 2 * tq, ATT_AUG), F32)],
        compiler_params=pltpu.CompilerParams(
            dimension_semantics=("arbitrary", "arbitrary"), vmem_limit_bytes=VMEM_LIMIT_BYTES),
        name="diffattn",
    )(slopes, lq1, lk1, lq2, lk2, subln_g, *([proj3d] * (3 * nh)))


def _diff_stage(proj3d, p):
    row = lambda t: t.reshape(1, -1)
    slopes = 2.0 ** (-8.0 * (jnp.arange(DIFF_HEADS, dtype=F32) + 1.0) / DIFF_HEADS)
    slopes = jnp.broadcast_to(slopes[:, None, None], (DIFF_HEADS, 1, LANES))
    return _diff(proj3d, slopes, row(p['diff_lq1'][0]), row(p['diff_lk1'][0]), row(p['diff_lq2'][0]),
                 row(p['diff_lk2'][0]), row(p['diff_subln_g'][0]))


PACK_COLS = D_MODEL // 2


def _pack_bf16_pairs(x):
    xb = x.astype(BF16).astype(F32)
    lo = lax.bitcast_convert_type(xb[:, :PACK_COLS], jnp.uint32) >> 16
    hi = lax.bitcast_convert_type(xb[:, PACK_COLS:], jnp.uint32)
    return lo | hi


def _unpack_bf16_pairs(xp):
    lo = lax.bitcast_convert_type(xp << 16, F32)
    hi = lax.bitcast_convert_type(xp & jnp.uint32(0xFFFF0000), F32)
    return jnp.concatenate([lo, hi], axis=1).astype(BF16)


def _mix_router_kernel(x_ref, yr_ref, yd_ref, wt_ref, wb_ref, g_ref, rwh_ref, rwl_ref, rb_ref,
                       h_ref, xp_ref, ti_ref, gt_ref, rk_ref, cnt_ref):
    tm = x_ref.shape[0]

    @pl.when(pl.program_id(0) == 0)
    def _():
        cnt_ref[...] = jnp.zeros_like(cnt_ref)

    h = (x_ref[...] + _dot(yr_ref[...].astype(BF16), wt_ref[...])
         + _dot(yd_ref[...].astype(BF16), wb_ref[...]))
    h_ref[...] = h
    xn = _rms(h, g_ref[...])
    xp_ref[...] = _pack_bf16_pairs(xn)
    x_hi, x_lo = _split2(xn)
    logits = (_dot(x_hi, rwh_ref[...]) + _dot(x_hi, rwl_ref[...]) + _dot(x_lo, rwh_ref[...])
              + rb_ref[...])

    lane_e = lax.broadcasted_iota(jnp.int32, (tm, N_EXPERTS), 1)
    lane_k = lax.broadcasted_iota(jnp.int32, (tm, TOP_K), 1)
    vals = logits
    tops, sels = [], []
    top_i = jnp.zeros((tm, TOP_K), jnp.int32)
    for k in range(TOP_K):
        m = jnp.max(vals, axis=-1, keepdims=True)
        idx = jnp.min(jnp.where(vals == m, lane_e, N_EXPERTS), axis=-1, keepdims=True)
        sel = lane_e == idx
        vals = jnp.where(sel, -jnp.inf, vals)
        tops.append(m)
        sels.append(sel)
        top_i = jnp.where(lane_k == k, idx, top_i)
    ti_ref[...] = top_i

    exps = [jnp.exp(m - tops[0]) for m in tops]
    den = exps[0] + exps[1] + exps[2] + exps[3]
    gates = jnp.zeros((tm, TOP_K), F32)
    for k in range(TOP_K):
        gates = jnp.where(lane_k == k, exps[k] / den, gates)
    gt_ref[...] = gates

    cnt = jnp.zeros((tm, N_EXPERTS), F32)
    for sel in sels:
        cnt = cnt + sel.astype(F32)
    ri = lax.broadcasted_iota(jnp.int32, (tm, tm), 0)
    rj = lax.broadcasted_iota(jnp.int32, (tm, tm), 1)
    before = (rj < ri).astype(BF16)
    prefix = _dot(before, cnt.astype(BF16)) + cnt_ref[...]
    rank = jnp.zeros((tm, TOP_K), F32)
    for k in range(TOP_K):
        rk = jnp.sum(jnp.where(sels[k], prefix, 0.0), axis=-1, keepdims=True)
        rank = jnp.where(lane_k == k, rk, rank)
    rk_ref[...] = rank.astype(jnp.int32)
    cnt_ref[...] += jnp.sum(cnt, axis=0, keepdims=True)


def _mix_router(x2d, yr2d, yd2d, w_top, w_bot, g2, rw_hi, rw_lo, rb):
    n_tok = x2d.shape[0]
    tm = ROW_TILE
    rows = lambda n: pl.BlockSpec((tm, n), lambda i: (i, 0))
    full = lambda m, n: pl.BlockSpec((m, n), lambda i: (0, 0))
    return pl.pallas_call(
        _mix_router_kernel,
        grid=(n_tok // tm,),
        in_specs=[rows(D_MODEL), rows(RWKV_WIDTH), rows(DIFF_WIDTH),
                  full(RWKV_WIDTH, D_MODEL), full(DIFF_WIDTH, D_MODEL), full(1, D_MODEL),
                  full(D_MODEL, N_EXPERTS), full(D_MODEL, N_EXPERTS), full(1, N_EXPERTS)],
        out_specs=[rows(D_MODEL), rows(PACK_COLS), rows(TOP_K), rows(TOP_K), rows(TOP_K),
                   full(1, N_EXPERTS)],
        out_shape=[jax.ShapeDtypeStruct((n_tok, D_MODEL), F32),
                   jax.ShapeDtypeStruct((n_tok, PACK_COLS), jnp.uint32),
                   jax.ShapeDtypeStruct((n_tok, TOP_K), jnp.int32),
                   jax.ShapeDtypeStruct((n_tok, TOP_K), F32),
                   jax.ShapeDtypeStruct((n_tok, TOP_K), jnp.int32),
                   jax.ShapeDtypeStruct((1, N_EXPERTS), F32)],
        compiler_params=pltpu.CompilerParams(
            dimension_semantics=("arbitrary",), vmem_limit_bytes=VMEM_LIMIT_BYTES),
        name="mix_router",
    )(x2d, yr2d, yd2d, w_top, w_bot, g2, rw_hi, rw_lo, rb)


EXPERT_TILE = 512


def _dispatch_kernel(dest_ref, xp_ref, init_ref, xs_ref, sem):
    del init_ref
    tm = xp_ref.shape[0]

    def issue(t, carry):
        for k in range(TOP_K):
            d = dest_ref[0, 0, t * TOP_K + k]
            pltpu.make_async_copy(xp_ref.at[pl.ds(t, 1), :], xs_ref.at[pl.ds(d, 1), :], sem).start()
        return carry

    lax.fori_loop(0, tm, issue, 0, unroll=8)
    for k in range(TOP_K):
        pltpu.make_async_copy(xp_ref, xs_ref.at[pl.ds(0, tm), :], sem).wait()


def _dispatch(dest3d, xp, xs_init):
    n_tok = xp.shape[0]
    tm = ROW_TILE
    return pl.pallas_call(
        _dispatch_kernel,
        grid=(n_tok // tm,),
        in_specs=[pl.BlockSpec((1, 1, tm * TOP_K), lambda i: (i, 0, 0), memory_space=pltpu.SMEM),
                  pl.BlockSpec((tm, PACK_COLS), lambda i: (i, 0)),
                  pl.BlockSpec(memory_space=pl.ANY)],
        out_specs=pl.BlockSpec(memory_space=pl.ANY),
        out_shape=jax.ShapeDtypeStruct(xs_init.shape, xs_init.dtype),
        scratch_shapes=[pltpu.SemaphoreType.DMA(())],
        input_output_aliases={2: 0},
        compiler_params=pltpu.CompilerParams(
            dimension_semantics=("arbitrary",), vmem_limit_bytes=VMEM_LIMIT_BYTES),
        name="dispatch",
    )(dest3d, xp, xs_init)


SPLIT_BLOCK = 2 * LANES
UP_COLS_PER_STEP = 1024


def _split_up_kernel(w_ref, wg_ref, wl_ref):
    src = lax.broadcasted_iota(jnp.int32, (SPLIT_BLOCK, SPLIT_BLOCK), 0)
    dst = lax.broadcasted_iota(jnp.int32, (SPLIT_BLOCK, SPLIT_BLOCK), 1)
    perm = (src == jnp.where(dst < LANES, 2 * dst, 2 * (dst - LANES) + 1)).astype(BF16)
    for c in range(UP_COLS_PER_STEP // SPLIT_BLOCK):
        blk = w_ref[0, :, c * SPLIT_BLOCK:(c + 1) * SPLIT_BLOCK].astype(BF16)
        out = _dot(blk, perm).astype(BF16)
        wg_ref[0, :, c * LANES:(c + 1) * LANES] = out[:, :LANES]
        wl_ref[0, :, c * LANES:(c + 1) * LANES] = out[:, LANES:]


def _split_up(w_up):
    n_exp, d_in, d_out2 = w_up.shape
    half = UP_COLS_PER_STEP // 2
    out = jax.ShapeDtypeStruct((n_exp, d_in, d_out2 // 2), BF16)
    return pl.pallas_call(
        _split_up_kernel,
        grid=(n_exp, d_out2 // UP_COLS_PER_STEP),
        in_specs=[pl.BlockSpec((1, d_in, UP_COLS_PER_STEP), lambda e, c: (e, 0, c))],
        out_specs=[pl.BlockSpec((1, d_in, half), lambda e, c: (e, 0, c)),
                   pl.BlockSpec((1, d_in, half), lambda e, c: (e, 0, c))],
        out_shape=[out, out],
        compiler_params=pltpu.CompilerParams(
            dimension_semantics=("arbitrary", "arbitrary"), vmem_limit_bytes=VMEM_LIMIT_BYTES),
        name="split_up",
    )(w_up)


def _expert_kernel(be_ref, src_ref, nu_ref, xs_ref, wg_ref, wl_ref, wd_ref, bg_ref, bl_ref, bd_ref, y_ref):
    del be_ref, src_ref
    used = pl.program_id(0) < nu_ref[0]

    @pl.when(jnp.logical_not(used))
    def _():
        y_ref[...] = jnp.zeros_like(y_ref)

    @pl.when(used)
    def _():
        x = _unpack_bf16_pairs(xs_ref[...])
        glu = jnp.minimum(_dot(x, wg_ref[0]) + bg_ref[0], SWIGLU_LIMIT)
        lin = jnp.clip(_dot(x, wl_ref[0]) + bl_ref[0], -SWIGLU_LIMIT, SWIGLU_LIMIT)
        act = glu * jax.nn.sigmoid(SWIGLU_ALPHA * glu) * (lin + 1.0)
        y_ref[...] = _dot(act.astype(BF16), wd_ref[0]) + bd_ref[0]


def _experts(blk_expert, blk_src, n_used, xs, wg, wl, wd, bg, bl, bd):
    n_slots = xs.shape[0]
    tb = EXPERT_TILE
    wspec = lambda k, n: pl.BlockSpec((1, k, n), lambda j, be, src, nu: (be[j], 0, 0))
    return pl.pallas_call(
        _expert_kernel,
        grid_spec=pltpu.PrefetchScalarGridSpec(
            num_scalar_prefetch=3,
            grid=(n_slots // tb,),
            in_specs=[pl.BlockSpec((tb, PACK_COLS), lambda j, be, src, nu: (src[j], 0)),
                      wspec(D_MODEL, D_FF), wspec(D_MODEL, D_FF), wspec(D_FF, D_MODEL),
                      wspec(1, D_FF), wspec(1, D_FF), wspec(1, D_MODEL)],
            out_specs=pl.BlockSpec((tb, D_MODEL), lambda j, be, src, nu: (j, 0)),
        ),
        out_shape=jax.ShapeDtypeStruct((n_slots, D_MODEL), F32),
        compiler_params=pltpu.CompilerParams(
            dimension_semantics=("arbitrary",), vmem_limit_bytes=VMEM_LIMIT_BYTES),
        name="experts",
    )(blk_expert, blk_src, n_used, xs, wg, wl, wd, bg, bl, bd)


COMBINE_TILE = 256


def _combine_kernel(dest_ref, h_ref, gt_ref, g_ref, ys_ref, o_ref, buf, sem):
    tm = h_ref.shape[0]

    def issue(t, carry):
        for k in range(TOP_K):
            d = dest_ref[0, 0, t * TOP_K + k]
            pltpu.make_async_copy(ys_ref.at[pl.ds(d, 1), :], buf.at[k, pl.ds(t, 1), :], sem).start()
        return carry

    lax.fori_loop(0, tm, issue, 0, unroll=8)
    for k in range(TOP_K):
        pltpu.make_async_copy(ys_ref.at[pl.ds(0, tm), :], buf.at[k], sem).wait()
    gates = gt_ref[...]
    out = h_ref[...]
    for k in range(TOP_K):
        out = out + gates[:, k:k + 1] * buf[k]
    o_ref[...] = _rms(out, g_ref[...])


def _combine(dest3d, h, gates, final_g, ys):
    n_tok = h.shape[0]
    tm = COMBINE_TILE
    return pl.pallas_call(
        _combine_kernel,
        grid=(n_tok // tm,),
        in_specs=[pl.BlockSpec((1, 1, tm * TOP_K), lambda i: (i, 0, 0), memory_space=pltpu.SMEM),
                  pl.BlockSpec((tm, D_MODEL), lambda i: (i, 0)),
                  pl.BlockSpec((tm, TOP_K), lambda i: (i, 0)),
                  pl.BlockSpec((1, D_MODEL), lambda i: (0, 0)),
                  pl.BlockSpec(memory_space=pl.ANY)],
        out_specs=pl.BlockSpec((tm, D_MODEL), lambda i: (i, 0)),
        out_shape=jax.ShapeDtypeStruct((n_tok, D_MODEL), F32),
        scratch_shapes=[pltpu.VMEM((TOP_K, tm, D_MODEL), F32), pltpu.SemaphoreType.DMA(())],
        compiler_params=pltpu.CompilerParams(
            dimension_semantics=("arbitrary",), vmem_limit_bytes=VMEM_LIMIT_BYTES),
        name="combine",
    )(dest3d, h, gates, final_g, ys)


def _moe_stage(x2d, yr2d, yd2d, p):
    n_tok = x2d.shape[0]
    row = lambda t: t.reshape(1, -1)
    w_out = p['w_out'][0].astype(BF16)
    rw = p['router_w'][0]
    rw_hi = rw.astype(BF16)
    rw_lo = (rw - rw_hi.astype(F32)).astype(BF16)
    h, xp, top_i, gates, rank, counts = _mix_router(
        x2d, yr2d, yd2d, w_out[:RWKV_WIDTH], w_out[RWKV_WIDTH:], row(p['norm2_g'][0]),
        rw_hi, rw_lo, row(p['router_b'][0]))

    tb = EXPERT_TILE
    n_blocks = (n_tok * TOP_K) // tb + N_EXPERTS
    counts = counts[0].astype(jnp.int32)
    padded = (counts + tb - 1) // tb * tb
    pad_end = jnp.cumsum(padded)
    pad_start = pad_end - padded
    dest = pad_start[top_i] + rank
    n_used = pad_end[-1] // tb
    blk_src = jnp.minimum(jnp.arange(n_blocks, dtype=jnp.int32), n_used - 1)
    blk_expert = jnp.minimum(
        jnp.searchsorted(pad_end, blk_src * tb, side='right'), N_EXPERTS - 1).astype(jnp.int32)

    xs_init = jnp.zeros((n_blocks * tb, PACK_COLS), jnp.uint32)
    xs = _dispatch(dest.reshape(n_tok // ROW_TILE, 1, ROW_TILE * TOP_K), xp, xs_init)

    w_up = p['exp_w_up'][0]
    b_up = p['exp_b_up'][0]
    wg, wl = _split_up(w_up)
    bg, bl = b_up[:, None, 0::2], b_up[:, None, 1::2]
    ys = _experts(blk_expert, blk_src.astype(jnp.int32), n_used.reshape(1).astype(jnp.int32), xs, wg, wl,
                  p['exp_w_down'][0].astype(BF16), bg, bl, p['exp_b_down'][0][:, None, :])
    return _combine(dest.reshape(n_tok // COMBINE_TILE, 1, COMBINE_TILE * TOP_K), h, gates,
                    row(p['final_g']), ys)


def _moe_stage_test(h2d, p):
    zeros = jnp.zeros((h2d.shape[0], RWKV_WIDTH), F32)
    return _moe_stage(h2d, zeros, zeros, p)


def kernel(x, norm1_g, w_in, rwkv_mu, rwkv_w0, rwkv_wb, rwkv_a0, rwkv_ab, rwkv_gb, rwkv_kk, rwkv_ka, rwkv_rk, rwkv_ln_g, rwkv_ln_b, diff_lq1, diff_lk1, diff_lq2, diff_lk2, diff_subln_g, w_out, norm2_g, router_w, router_b, exp_w_up, exp_b_up, exp_w_down, exp_b_down, final_g):
    p = dict(rwkv_mu=rwkv_mu, rwkv_w0=rwkv_w0, rwkv_wb=rwkv_wb, rwkv_a0=rwkv_a0, rwkv_ab=rwkv_ab,
             rwkv_gb=rwkv_gb, rwkv_kk=rwkv_kk, rwkv_ka=rwkv_ka, rwkv_rk=rwkv_rk,
             rwkv_ln_g=rwkv_ln_g, rwkv_ln_b=rwkv_ln_b, diff_lq1=diff_lq1, diff_lk1=diff_lk1,
             diff_lq2=diff_lq2, diff_lk2=diff_lk2, diff_subln_g=diff_subln_g, w_out=w_out,
             norm2_g=norm2_g, router_w=router_w, router_b=router_b, exp_w_up=exp_w_up,
             exp_b_up=exp_b_up, exp_w_down=exp_w_down, exp_b_down=exp_b_down, final_g=final_g)
    bsz, seq, d = x.shape
    n_tok = bsz * seq
    x2d = x.reshape(n_tok, d)
    proj = _inproj(x2d, norm1_g, w_in[0].astype(BF16)).reshape(bsz, seq, IN_COLS)
    y_rwkv = _rwkv_stage(proj, p).reshape(n_tok, RWKV_WIDTH)
    y_diff = _diff_stage(proj, p).reshape(n_tok, DIFF_WIDTH)
    return _moe_stage(x2d, y_rwkv, y_diff, p).reshape(bsz, seq, d)
```

```python
import functools
import math

import jax
import jax.numpy as jnp
from jax import lax
from jax.experimental import pallas as pl
from jax.experimental.pallas import tpu as pltpu

F32 = jnp.float32
BF16 = jnp.bfloat16

D_MODEL = 1024
RWKV_WIDTH = 512
RWKV_HEAD = 64
DECAY_LORA = 64
ICL_LORA = 64
GATE_LORA = 128
GN_EPS = 64e-5
DIFF_WIDTH = 512
DIFF_HEAD = 64
DIFF_HEADS = 4
DIFF_VDIM = 128
RWKV_COLS = 3 * RWKV_WIDTH + DECAY_LORA + ICL_LORA + GATE_LORA
IN_COLS = RWKV_COLS + 3 * DIFF_WIDTH
N_EXPERTS = 32
TOP_K = 4
D_FF = 1024
SWIGLU_ALPHA = 1.702
SWIGLU_LIMIT = 7.0
NORM_EPS = 1e-5
LAMBDA_INIT = 0.8 - 0.6 * math.exp(-0.0)

LANES = 128
VMEM_LIMIT_BYTES = 56 * 1024 * 1024

ROW_TILE = 512
RWKV_CHUNK = 64
RWKV_GROUP = 4
GROUP_LANES = RWKV_GROUP * RWKV_HEAD
PREP_UNROLL = 2


def _dot(a, b):
    return jnp.dot(a, b, preferred_element_type=F32)


def _dot_nt(a, b):
    return lax.dot_general(a, b, (((1,), (1,)), ((), ())), preferred_element_type=F32)


def _dot_tn(a, b):
    return lax.dot_general(a, b, (((0,), (0,)), ((), ())), preferred_element_type=F32)


def _split2(x):
    hi = x.astype(BF16)
    lo = (x - hi.astype(F32)).astype(BF16)
    return hi, lo


def _rms(x, g):
    return x * lax.rsqrt(jnp.mean(x * x, axis=-1, keepdims=True) + NORM_EPS) * g


def _inproj_kernel(x_ref, g_ref, w_ref, o_ref):
    xn = _rms(x_ref[...], g_ref[...])
    o_ref[...] = _dot(xn.astype(BF16), w_ref[...])


def _inproj(x2d, g, w_bf16):
    n_tok = x2d.shape[0]
    return pl.pallas_call(
        _inproj_kernel,
        grid=(n_tok // ROW_TILE,),
        in_specs=[
            pl.BlockSpec((ROW_TILE, D_MODEL), lambda i: (i, 0)),
            pl.BlockSpec((1, D_MODEL), lambda i: (0, 0)),
            pl.BlockSpec((D_MODEL, IN_COLS), lambda i: (0, 0)),
        ],
        out_specs=pl.BlockSpec((ROW_TILE, IN_COLS), lambda i: (i, 0)),
        out_shape=jax.ShapeDtypeStruct((n_tok, IN_COLS), F32),
        compiler_params=pltpu.CompilerParams(
            dimension_semantics=("arbitrary",), vmem_limit_bytes=VMEM_LIMIT_BYTES),
        name="inproj",
    )(x2d, g, w_bf16)


def _head_sum(x, bd_ones):
    xb = x.astype(BF16)
    return jnp.concatenate(
        [_dot(xb[:, g * GROUP_LANES:(g + 1) * GROUP_LANES], bd_ones)
         for g in range(RWKV_WIDTH // GROUP_LANES)], axis=1)


def _rwkv_kernel(p_ref, mu_ref, w0_ref, wb_ref, a0_ref, ab_ref, gb_ref, kkw_ref, ka_ref,
                 rk_ref, lng_ref, lnb_ref, o_ref,
                 carry_ref, state_ref, r_s, k_s, v_s, kk_s, b_s, lw_s, y_s,
                 lhs_s, w0_s, yloc_s, arb_s, kb_s, dec_s):
    ts = p_ref.shape[1]
    n_chunks = ts // RWKV_CHUNK
    n_groups = RWKV_WIDTH // GROUP_LANES

    @pl.when(pl.program_id(1) == 0)
    def _():
        carry_ref[...] = jnp.zeros_like(carry_ref)
        state_ref[...] = jnp.zeros_like(state_ref)

    row = lax.broadcasted_iota(jnp.int32, (ts, 1), 0)

    def shifted(c0, c1):
        x = p_ref[0, :, c0:c1]
        prev = pltpu.roll(x, 1, 0)
        prev = jnp.where(row == 0, carry_ref[:, c0:c1], prev)
        return x + (prev - x) * mu_ref[:, c0:c1]

    w1, w2, w3 = RWKV_WIDTH, 2 * RWKV_WIDTH, 3 * RWKV_WIDTH
    r = shifted(0, w1)
    k = shifted(w1, w2)
    v = shifted(w2, w3)
    wa_lo = shifted(w3, w3 + DECAY_LORA + ICL_LORA)
    g_lo = shifted(w3 + DECAY_LORA + ICL_LORA, RWKV_COLS)
    carry_ref[...] = p_ref[0, ts - 1:ts, :]

    gi = lax.broadcasted_iota(jnp.int32, (GROUP_LANES, GROUP_LANES), 0)
    gj = lax.broadcasted_iota(jnp.int32, (GROUP_LANES, GROUP_LANES), 1)
    same_head = (gi // RWKV_HEAD) == (gj // RWKV_HEAD)
    strict = same_head & (gi % RWKV_CHUNK > gj % RWKV_CHUNK)
    incl = same_head & (gi % RWKV_CHUNK >= gj % RWKV_CHUNK)
    eye = (gi == gj).astype(F32)
    bd_ones = same_head.astype(BF16)

    z = w0_ref[...] + _dot(jnp.tanh(wa_lo).astype(BF16), wb_ref[...])
    nz = -z
    softplus = jnp.maximum(nz, 0.0) + jnp.log1p(jnp.exp(-jnp.abs(nz)))
    lw = -jnp.exp(-softplus - 0.5)
    a = jax.nn.sigmoid(a0_ref[...] + _dot(wa_lo.astype(BF16), ab_ref[...]))
    gate = _dot(jax.nn.sigmoid(g_lo).astype(BF16), gb_ref[...])
    kk = k * kkw_ref[...]
    kk = kk * lax.rsqrt(jnp.maximum(_head_sum(kk * kk, bd_ones), 1e-24))
    k2 = k * (1.0 + (a - 1.0) * ka_ref[...])
    r_s[...] = r
    k_s[...] = k2
    v_s[...] = v
    kk_s[...] = kk
    b_s[...] = kk * a
    lw_s[...] = lw

    ci = lax.broadcasted_iota(jnp.int32, (RWKV_CHUNK, RWKV_CHUNK), 0)
    cj = lax.broadcasted_iota(jnp.int32, (RWKV_CHUNK, RWKV_CHUNK), 1)
    tri = (ci >= cj).astype(BF16)
    tri2 = jnp.concatenate([tri, tri], axis=1)

    def tile_heads(x):
        return jnp.concatenate([x] * RWKV_GROUP, axis=0)

    def fold_heads(x):
        out = x[0:RWKV_CHUNK]
        for i in range(1, RWKV_GROUP):
            out = out + x[i * RWKV_CHUNK:(i + 1) * RWKV_CHUNK]
        return out

    def chunk_rows(c):
        return slice(c * RWKV_CHUNK, (c + 1) * RWKV_CHUNK)

    def prepare(cp):
        units = [(cp * PREP_UNROLL + i, g) for i in range(PREP_UNROLL) for g in range(n_groups)]
        lanes = [slice(g * GROUP_LANES, (g + 1) * GROUP_LANES) for _, g in units]
        rows = [chunk_rows(c) for c, _ in units]
        load = lambda ref: [ref[rw, ln] for rw, ln in zip(rows, lanes)]
        rc, kc, vc, kkc, bc, lwc = (load(ref) for ref in (r_s, k_s, v_s, kk_s, b_s, lw_s))
        cum = [_dot(tri2, jnp.concatenate(_split2(x), axis=0)) for x in lwc]
        yield
        cum_end = [x[RWKV_CHUNK - 1:RWKV_CHUNK, :] for x in cum]
        for (c, _), ln, ce in zip(units, lanes, cum_end):
            dec_s[c, :, ln] = jnp.exp(ce)
        e_neg = [jnp.exp(-x) for x in cum]
        e_end = [jnp.exp(ce - x) for ce, x in zip(cum_end, cum)]
        kkd = [a * jnp.exp(x - lw) for a, x, lw in zip(kkc, cum, lwc)]
        rd = [a * jnp.exp(x) for a, x in zip(rc, cum)]
        spread = lambda xs: [jnp.where(same_head, tile_heads(x), 0.0).astype(BF16) for x in xs]
        kkd_s, rd_s, v_st = spread(kkd), spread(rd), spread(vc)
        rhs = [jnp.concatenate([tile_heads(b * e), tile_heads(k * e)], axis=0).astype(BF16)
               for b, k, e in zip(bc, kc, e_neg)]
        yield
        gram = [_dot_nt(jnp.concatenate([a, b], axis=0), w) for a, b, w in zip(kkd_s, rd_s, rhs)]
        yield
        gl = GROUP_LANES
        a_ab = [jnp.where(strict, x[:gl, :gl], 0.0) for x in gram]
        a_rb = [jnp.where(incl, x[gl:, :gl], 0.0).astype(BF16) for x in gram]
        a_ak = [jnp.where(strict, x[:gl, gl:], 0.0).astype(BF16) for x in gram]
        a_rk = [jnp.where(incl, x[gl:, gl:], 0.0).astype(BF16) for x in gram]
        av = [_dot(jnp.concatenate([a, b], axis=0), v) for a, b, v in zip(a_ak, a_rk, v_st)]
        yield

        pw = [x.astype(BF16) for x in a_ab]
        t_inv = [eye - x for x in a_ab]
        for _ in range(int(math.log2(RWKV_CHUNK)) - 1):
            pw = [_dot(x, x).astype(BF16) for x in pw]
            t_inv = [t + _dot(t.astype(BF16), x) for t, x in zip(t_inv, pw)]
            yield
        sol = [_dot(t.astype(BF16), jnp.concatenate([a, x[:gl].astype(BF16)], axis=1))
               for t, a, x in zip(t_inv, kkd_s, av)]
        yield
        for i, (c, g) in enumerate(units):
            lhs_s[c, g] = jnp.concatenate([sol[i][:, :gl].astype(BF16), rd_s[i]], axis=0)
            w0_s[c, g] = sol[i][:, gl:]
            yloc_s[c, g] = fold_heads(av[i][gl:])
            arb_s[c, g] = a_rb[i]
            kb_s[c, g] = jnp.concatenate([kc[i] * e_end[i], bc[i] * e_end[i]], axis=0).astype(BF16)

    def advance(c):
        rows = chunk_rows(c)
        gs = range(n_groups)
        lanes = [slice(g * GROUP_LANES, (g + 1) * GROUP_LANES) for g in gs]
        state = [state_ref[g] for g in gs]
        m1 = [_dot_nt(lhs_s[c, g], state[g].astype(BF16)) for g in gs]
        yield
        sa_st = [m1[g][:GROUP_LANES] + w0_s[c, g] for g in gs]
        y_st = [m1[g][GROUP_LANES:] - _dot(arb_s[c, g], sa_st[g].astype(BF16)) for g in gs]
        yield
        upd = [_dot_tn(jnp.concatenate([v_s[rows, lanes[g]], -fold_heads(sa_st[g])], axis=0).astype(BF16),
                       kb_s[c, g]) for g in gs]
        yield
        for g in gs:
            y_s[rows, lanes[g]] = fold_heads(y_st[g]) + yloc_s[c, g]
            state_ref[g] = state[g] * dec_s[c, :, lanes[g]] + jnp.where(same_head, upd[g], 0.0)
        yield

    def advance_chunks(cp):
        for i in range(PREP_UNROLL):
            yield from advance(cp * PREP_UNROLL + i)

    def interleave(*stages):
        live = list(stages)
        while live:
            for gen in list(live):
                if next(gen, StopIteration) is StopIteration:
                    live.remove(gen)

    n_pairs = n_chunks // PREP_UNROLL
    interleave(prepare(0))
    for cp in range(n_pairs):
        ahead = [prepare(cp + 1)] if cp + 1 < n_pairs else []
        interleave(*ahead, advance_chunks(cp))

    y = y_s[...]
    inv_n = 1.0 / RWKV_HEAD
    mean = _head_sum(y, bd_ones) * inv_n
    d = y - mean
    var = _head_sum(d * d, bd_ones) * inv_n
    yn = d * lax.rsqrt(var + GN_EPS) * lng_ref[...] + lnb_ref[...]
    bonus = _head_sum(r_s[...] * k_s[...] * rk_ref[...], bd_ones) * v_s[...]
    o_ref[0] = (yn + bonus) * gate


def _rwkv(proj3d, mu, w0, wb_pad, a0, ab_pad, gb, kkw, ka, rk, lng, lnb):
    bsz, seq, _ = proj3d.shape
    ts = ROW_TILE
    vec = lambda n: pl.BlockSpec((1, n), lambda b, s: (0, 0))
    mat = lambda m, n: pl.BlockSpec((m, n), lambda b, s: (0, 0))
    tile_f32 = pltpu.VMEM((ts, RWKV_WIDTH), F32)
    n_chunks = ts // RWKV_CHUNK
    n_groups = RWKV_WIDTH // GROUP_LANES
    return pl.pallas_call(
        _rwkv_kernel,
        grid=(bsz, seq // ts),
        in_specs=[
            pl.BlockSpec((1, ts, RWKV_COLS), lambda b, s: (b, s, 0)),
            vec(RWKV_COLS), vec(RWKV_WIDTH), mat(DECAY_LORA + ICL_LORA, RWKV_WIDTH),
            vec(RWKV_WIDTH), mat(DECAY_LORA + ICL_LORA, RWKV_WIDTH), mat(GATE_LORA, RWKV_WIDTH),
            vec(RWKV_WIDTH), vec(RWKV_WIDTH), vec(RWKV_WIDTH), vec(RWKV_WIDTH), vec(RWKV_WIDTH),
        ],
        out_specs=pl.BlockSpec((1, ts, RWKV_WIDTH), lambda b, s: (b, s, 0)),
        out_shape=jax.ShapeDtypeStruct((bsz, seq, RWKV_WIDTH), F32),
        scratch_shapes=[
            pltpu.VMEM((1, RWKV_COLS), F32),
            pltpu.VMEM((n_groups, GROUP_LANES, GROUP_LANES), F32),
            tile_f32, tile_f32, tile_f32, tile_f32, tile_f32, tile_f32, tile_f32,
            pltpu.VMEM((n_chunks, n_groups, 2 * GROUP_LANES, GROUP_LANES), BF16),
            pltpu.VMEM((n_chunks, n_groups, GROUP_LANES, GROUP_LANES), F32),
            pltpu.VMEM((n_chunks, n_groups, RWKV_CHUNK, GROUP_LANES), F32),
            pltpu.VMEM((n_chunks, n_groups, GROUP_LANES, GROUP_LANES), BF16),
            pltpu.VMEM((n_chunks, n_groups, 2 * RWKV_CHUNK, GROUP_LANES), BF16),
            pltpu.VMEM((n_chunks, 1, RWKV_WIDTH), F32),
        ],
        compiler_params=pltpu.CompilerParams(
            dimension_semantics=("arbitrary", "arbitrary"), vmem_limit_bytes=VMEM_LIMIT_BYTES),
        name="rwkv7",
    )(proj3d, mu, w0, wb_pad, a0, ab_pad, gb, kkw, ka, rk, lng, lnb)


def _rwkv_stage(proj3d, p):
    row = lambda t: t.reshape(1, -1)
    zeros = jnp.zeros((DECAY_LORA, RWKV_WIDTH), F32)
    wb_pad = jnp.concatenate([p['rwkv_wb'][0], zeros], axis=0).astype(BF16)
    ab_pad = jnp.concatenate([zeros, p['rwkv_ab'][0]], axis=0).astype(BF16)
    return _rwkv(proj3d, row(p['rwkv_mu'][0]), row(p['rwkv_w0'][0]), wb_pad, row(p['rwkv_a0'][0]), ab_pad,
                 p['rwkv_gb'][0].astype(BF16), row(p['rwkv_kk'][0]), row(p['rwkv_ka'][0]),
                 row(p['rwkv_rk'][0]), row(p['rwkv_ln_g'][0]), row(p['rwkv_ln_b'][0]))


ATT_TILE = 256
ATT_LOCKSTEP = 2
ATT_AUG = 2 * LANES
POS_SPLIT_BITS = 6
MASK_VALUE = -1e30


def _alibi_lanes(n, pos0, slope, key_side):
    pos = pos0 + lax.broadcasted_iota(jnp.int32, (n, LANES), 0)
    lane = lax.broadcasted_iota(jnp.int32, (n, LANES), 1)
    hi = (pos >> POS_SPLIT_BITS).astype(F32) * (slope * float(1 << POS_SPLIT_BITS))
    lo = (pos & ((1 << POS_SPLIT_BITS) - 1)).astype(F32) * slope
    if key_side:
        return jnp.where(lane == 0, hi, jnp.where(lane == 1, lo, jnp.where(lane < 4, 1.0, 0.0)))
    return jnp.where(lane < 2, 1.0, jnp.where(lane == 2, -hi, jnp.where(lane == 3, -lo, 0.0)))


def _diff_kernel(slope_ref, lq1_ref, lk1_ref, lq2_ref, lk2_ref, g_ref, *refs):
    nh = DIFF_HEADS
    q_refs, k_refs, v_refs = refs[:nh], refs[nh:2 * nh], refs[2 * nh:3 * nh]
    o_ref, ks_ref, vs_ref, m_ref, acc_ref = refs[3 * nh:]
    tq = q_refs[0].shape[1]
    seq = k_refs[0].shape[1]
    qi = pl.program_id(1)
    slopes = [slope_ref[h][:, :1] for h in range(nh)]

    @pl.when(qi == 0)
    def _():
        for h in range(nh):
            ks_ref[h, :, :LANES] = k_refs[h][0].astype(BF16)
            ks_ref[h, :, LANES:] = _alibi_lanes(seq, 0, slopes[h], True).astype(BF16)
            vs_ref[h, :, :LANES] = v_refs[h][0].astype(BF16)
            vs_ref[h, :, LANES:] = jnp.ones((seq, LANES), BF16)

    lane = lax.broadcasted_iota(jnp.int32, (1, LANES), 1)
    qa = []
    for h in range(nh):
        q = q_refs[h][0] * (DIFF_HEAD ** -0.5)
        q_terms = _alibi_lanes(tq, qi * tq, slopes[h], False)
        qa.append(jnp.concatenate(
            [jnp.concatenate([jnp.where(lane < DIFF_HEAD, q, 0.0), q_terms], axis=1),
             jnp.concatenate([jnp.where(lane >= DIFF_HEAD, q, 0.0), q_terms], axis=1)],
            axis=0).astype(BF16))
    m_ref[...] = jnp.full_like(m_ref, MASK_VALUE)
    acc_ref[...] = jnp.zeros_like(acc_ref)
    causal = (lax.broadcasted_iota(jnp.int32, (tq, tq), 0)
              >= lax.broadcasted_iota(jnp.int32, (tq, tq), 1))
    causal = jnp.concatenate([causal, causal], axis=0)

    def process(j, diagonal):
        rows = pl.ds(pl.multiple_of(j * tq, tq), tq)
        for h0 in range(0, nh, ATT_LOCKSTEP):
            hs = range(h0, h0 + ATT_LOCKSTEP)
            s = {h: _dot_nt(qa[h], ks_ref[h, rows, :]) for h in hs}
            if diagonal:
                s = {h: jnp.where(causal, s[h], MASK_VALUE) for h in hs}
            m_old = {h: m_ref[h] for h in hs}
            m_new = {h: jnp.maximum(m_old[h], jnp.max(s[h], axis=-1, keepdims=True)) for h in hs}
            pr = {h: jnp.exp(s[h] - jnp.concatenate([m_new[h], m_new[h]], axis=1)).astype(BF16)
                  for h in hs}
            pv = {h: _dot(pr[h], vs_ref[h, rows, :]) for h in hs}
            for h in hs:
                alpha = jnp.exp(m_old[h] - m_new[h])
                acc_ref[h] = jnp.concatenate([alpha, alpha], axis=1) * acc_ref[h] + pv[h]
                m_ref[h] = m_new[h]

    def off_diagonal(j, carry):
        process(j, False)
        return carry

    lax.fori_loop(0, qi, off_diagonal, 0)
    process(qi, True)

    lam = (jnp.exp(jnp.sum(lq1_ref[...] * lk1_ref[...], axis=-1, keepdims=True))
           - jnp.exp(jnp.sum(lq2_ref[...] * lk2_ref[...], axis=-1, keepdims=True)) + LAMBDA_INIT)
    for h in range(nh):
        acc = acc_ref[h]
        o12 = acc[:, :DIFF_VDIM] / acc[:, DIFF_VDIM:]
        o = o12[:tq] - lam * o12[tq:]
        o_ref[0, :, h * DIFF_VDIM:(h + 1) * DIFF_VDIM] = _rms(o, g_ref[...]) * (1.0 - LAMBDA_INIT)


def _diff(proj3d, slopes, lq1, lk1, lq2, lk2, subln_g):
    bsz, seq, _ = proj3d.shape
    tq = ATT_TILE
    q_blk0 = RWKV_COLS // DIFF_VDIM
    k_blk0 = q_blk0 + DIFF_HEADS
    v_blk0 = k_blk0 + DIFF_HEADS
    nh = DIFF_HEADS
    vec = lambda n: pl.BlockSpec((1, n), lambda b, i: (0, 0))
    q_spec = lambda h: pl.BlockSpec((1, tq, DIFF_VDIM), lambda b, i: (b, i, q_blk0 + h))
    kv_spec = lambda blk: pl.BlockSpec((1, seq, DIFF_VDIM), lambda b, i: (b, 0, blk))
    return pl.pallas_call(
        _diff_kernel,
        grid=(bsz, seq // tq),
        in_specs=([pl.BlockSpec((nh, 1, LANES), lambda b, i: (0, 0, 0)),
                   vec(DIFF_HEAD), vec(DIFF_HEAD), vec(DIFF_HEAD), vec(DIFF_HEAD), vec(DIFF_VDIM)]
                  + [q_spec(h) for h in range(nh)]
                  + [kv_spec(k_blk0 + h) for h in range(nh)]
                  + [kv_spec(v_blk0 + h) for h in range(nh)]),
        out_specs=pl.BlockSpec((1, tq, DIFF_WIDTH), lambda b, i: (b, i, 0)),
        out_shape=jax.ShapeDtypeStruct((bsz, seq, DIFF_WIDTH), F32),
        scratch_shapes=[pltpu.VMEM((nh, seq, ATT_AUG), BF16), pltpu.VMEM((nh, seq, ATT_AUG), BF16),
                        pltpu.VMEM((nh, 2 * tq, LANES), F32), pltpu.VMEM((nh, 2 * tq, ATT_AUG), F32)],
        compiler_params=pltpu.CompilerParams(
            dimension_semantics=("arbitrary", "arbitrary"), vmem_limit_bytes=VMEM_LIMIT_BYTES),
        name="diffattn",
    )(slopes, lq1, lk1, lq2, lk2, subln_g, *([proj3d] * (3 * nh)))


def _diff_stage(proj3d, p):
    row = lambda t: t.reshape(1, -1)
    slopes = 2.0 ** (-8.0 * (jnp.arange(DIFF_HEADS, dtype=F32) + 1.0) / DIFF_HEADS)
    slopes = jnp.broadcast_to(slopes[:, None, None], (DIFF_HEADS, 1, LANES))
    return _diff(proj3d, slopes, row(p['diff_lq1'][0]), row(p['diff_lk1'][0]), row(p['diff_lq2'][0]),
                 row(p['diff_lk2'][0]), row(p['diff_subln_g'][0]))


PACK_COLS = D_MODEL // 2


def _pack_bf16_pairs(x):
    xb = x.astype(BF16).astype(F32)
    lo = lax.bitcast_convert_type(xb[:, :PACK_COLS], jnp.uint32) >> 16
    hi = lax.bitcast_convert_type(xb[:, PACK_COLS:], jnp.uint32)
    return lo | hi


def _unpack_bf16_pairs(xp):
    lo = lax.bitcast_convert_type(xp << 16, F32)
    hi = lax.bitcast_convert_type(xp & jnp.uint32(0xFFFF0000), F32)
    return jnp.concatenate([lo, hi], axis=1).astype(BF16)


def _mix_router_kernel(x_ref, yr_ref, yd_ref, wt_ref, wb_ref, g_ref, rwh_ref, rwl_ref, rb_ref,
                       h_ref, xp_ref, ti_ref, gt_ref, rk_ref, cnt_ref):
    tm = x_ref.shape[0]

    @pl.when(pl.program_id(0) == 0)
    def _():
        cnt_ref[...] = jnp.zeros_like(cnt_ref)

    h = (x_ref[...] + _dot(yr_ref[...].astype(BF16), wt_ref[...])
         + _dot(yd_ref[...].astype(BF16), wb_ref[...]))
    h_ref[...] = h
    xn = _rms(h, g_ref[...])
    xp_ref[...] = _pack_bf16_pairs(xn)
    x_hi, x_lo = _split2(xn)
    logits = (_dot(x_hi, rwh_ref[...]) + _dot(x_hi, rwl_ref[...]) + _dot(x_lo, rwh_ref[...])
              + rb_ref[...])

    lane_e = lax.broadcasted_iota(jnp.int32, (tm, N_EXPERTS), 1)
    lane_k = lax.broadcasted_iota(jnp.int32, (tm, TOP_K), 1)
    vals = logits
    tops, sels = [], []
    top_i = jnp.zeros((tm, TOP_K), jnp.int32)
    for k in range(TOP_K):
        m = jnp.max(vals, axis=-1, keepdims=True)
        idx = jnp.min(jnp.where(vals == m, lane_e, N_EXPERTS), axis=-1, keepdims=True)
        sel = lane_e == idx
        vals = jnp.where(sel, -jnp.inf, vals)
        tops.append(m)
        sels.append(sel)
        top_i = jnp.where(lane_k == k, idx, top_i)
    ti_ref[...] = top_i

    exps = [jnp.exp(m - tops[0]) for m in tops]
    den = exps[0] + exps[1] + exps[2] + exps[3]
    gates = jnp.zeros((tm, TOP_K), F32)
    for k in range(TOP_K):
        gates = jnp.where(lane_k == k, exps[k] / den, gates)
    gt_ref[...] = gates

    cnt = jnp.zeros((tm, N_EXPERTS), F32)
    for sel in sels:
        cnt = cnt + sel.astype(F32)
    ri = lax.broadcasted_iota(jnp.int32, (tm, tm), 0)
    rj = lax.broadcasted_iota(jnp.int32, (tm, tm), 1)
    before = (rj < ri).astype(BF16)
    prefix = _dot(before, cnt.astype(BF16)) + cnt_ref[...]
    rank = jnp.zeros((tm, TOP_K), F32)
    for k in range(TOP_K):
        rk = jnp.sum(jnp.where(sels[k], prefix, 0.0), axis=-1, keepdims=True)
        rank = jnp.where(lane_k == k, rk, rank)
    rk_ref[...] = rank.astype(jnp.int32)
    cnt_ref[...] += jnp.sum(cnt, axis=0, keepdims=True)


def _mix_router(x2d, yr2d, yd2d, w_top, w_bot, g2, rw_hi, rw_lo, rb):
    n_tok = x2d.shape[0]
    tm = ROW_TILE
    rows = lambda n: pl.BlockSpec((tm, n), lambda i: (i, 0))
    full = lambda m, n: pl.BlockSpec((m, n), lambda i: (0, 0))
    return pl.pallas_call(
        _mix_router_kernel,
        grid=(n_tok // tm,),
        in_specs=[rows(D_MODEL), rows(RWKV_WIDTH), rows(DIFF_WIDTH),
                  full(RWKV_WIDTH, D_MODEL), full(DIFF_WIDTH, D_MODEL), full(1, D_MODEL),
                  full(D_MODEL, N_EXPERTS), full(D_MODEL, N_EXPERTS), full(1, N_EXPERTS)],
        out_specs=[rows(D_MODEL), rows(PACK_COLS), rows(TOP_K), rows(TOP_K), rows(TOP_K),
                   full(1, N_EXPERTS)],
        out_shape=[jax.ShapeDtypeStruct((n_tok, D_MODEL), F32),
                   jax.ShapeDtypeStruct((n_tok, PACK_COLS), jnp.uint32),
                   jax.ShapeDtypeStruct((n_tok, TOP_K), jnp.int32),
                   jax.ShapeDtypeStruct((n_tok, TOP_K), F32),
                   jax.ShapeDtypeStruct((n_tok, TOP_K), jnp.int32),
                   jax.ShapeDtypeStruct((1, N_EXPERTS), F32)],
        compiler_params=pltpu.CompilerParams(
            dimension_semantics=("arbitrary",), vmem_limit_bytes=VMEM_LIMIT_BYTES),
        name="mix_router",
    )(x2d, yr2d, yd2d, w_top, w_bot, g2, rw_hi, rw_lo, rb)


EXPERT_TILE = 512
DISPATCH_TILE = 2048


def _dispatch_kernel(dest_ref, xp_ref, init_ref, xs_ref, sem):
    del init_ref
    tm = xp_ref.shape[0]

    def issue(t, carry):
        for k in range(TOP_K):
            d = dest_ref[0, 0, t * TOP_K + k]
            pltpu.make_async_copy(xp_ref.at[pl.ds(t, 1), :], xs_ref.at[pl.ds(d, 1), :], sem).start()
        return carry

    lax.fori_loop(0, tm, issue, 0, unroll=8)
    for k in range(TOP_K):
        pltpu.make_async_copy(xp_ref, xs_ref.at[pl.ds(0, tm), :], sem).wait()


def _dispatch(dest3d, xp, xs_init):
    n_tok = xp.shape[0]
    tm = DISPATCH_TILE
    return pl.pallas_call(
        _dispatch_kernel,
        grid=(n_tok // tm,),
        in_specs=[pl.BlockSpec((1, 1, tm * TOP_K), lambda i: (i, 0, 0), memory_space=pltpu.SMEM),
                  pl.BlockSpec((tm, PACK_COLS), lambda i: (i, 0)),
                  pl.BlockSpec(memory_space=pl.ANY)],
        out_specs=pl.BlockSpec(memory_space=pl.ANY),
        out_shape=jax.ShapeDtypeStruct(xs_init.shape, xs_init.dtype),
        scratch_shapes=[pltpu.SemaphoreType.DMA(())],
        input_output_aliases={2: 0},
        compiler_params=pltpu.CompilerParams(
            dimension_semantics=("arbitrary",), vmem_limit_bytes=VMEM_LIMIT_BYTES),
        name="dispatch",
    )(dest3d, xp, xs_init)


SPLIT_BLOCK = 2 * LANES
UP_COLS_PER_STEP = 1024


def _split_up_kernel(w_ref, wg_ref, wl_ref):
    src = lax.broadcasted_iota(jnp.int32, (SPLIT_BLOCK, SPLIT_BLOCK), 0)
    dst = lax.broadcasted_iota(jnp.int32, (SPLIT_BLOCK, SPLIT_BLOCK), 1)
    perm = (src == jnp.where(dst < LANES, 2 * dst, 2 * (dst - LANES) + 1)).astype(BF16)
    for c in range(UP_COLS_PER_STEP // SPLIT_BLOCK):
        blk = w_ref[0, :, c * SPLIT_BLOCK:(c + 1) * SPLIT_BLOCK].astype(BF16)
        out = _dot(blk, perm).astype(BF16)
        wg_ref[0, :, c * LANES:(c + 1) * LANES] = out[:, :LANES]
        wl_ref[0, :, c * LANES:(c + 1) * LANES] = out[:, LANES:]


def _split_up(w_up):
    n_exp, d_in, d_out2 = w_up.shape
    half = UP_COLS_PER_STEP // 2
    out = jax.ShapeDtypeStruct((n_exp, d_in, d_out2 // 2), BF16)
    return pl.pallas_call(
        _split_up_kernel,
        grid=(n_exp, d_out2 // UP_COLS_PER_STEP),
        in_specs=[pl.BlockSpec((1, d_in, UP_COLS_PER_STEP), lambda e, c: (e, 0, c))],
        out_specs=[pl.BlockSpec((1, d_in, half), lambda e, c: (e, 0, c)),
                   pl.BlockSpec((1, d_in, half), lambda e, c: (e, 0, c))],
        out_shape=[out, out],
        compiler_params=pltpu.CompilerParams(
            dimension_semantics=("arbitrary", "arbitrary"), vmem_limit_bytes=VMEM_LIMIT_BYTES),
        name="split_up",
    )(w_up)


def _expert_kernel(be_ref, src_ref, nu_ref, xs_ref, wg_ref, wl_ref, wd_ref, bg_ref, bl_ref, bd_ref, y_ref):
    del be_ref, src_ref
    used = pl.program_id(0) < nu_ref[0]

    @pl.when(jnp.logical_not(used))
    def _():
        y_ref[...] = jnp.zeros_like(y_ref)

    @pl.when(used)
    def _():
        x = _unpack_bf16_pairs(xs_ref[...])
        glu = jnp.minimum(_dot(x, wg_ref[0]) + bg_ref[0], SWIGLU_LIMIT)
        lin = jnp.clip(_dot(x, wl_ref[0]) + bl_ref[0], -SWIGLU_LIMIT, SWIGLU_LIMIT)
        act = glu * jax.nn.sigmoid(SWIGLU_ALPHA * glu) * (lin + 1.0)
        y_ref[...] = _dot(act.astype(BF16), wd_ref[0]) + bd_ref[0]


def _experts(blk_expert, blk_src, n_used, xs, wg, wl, wd, bg, bl, bd):
    n_slots = xs.shape[0]
    tb = EXPERT_TILE
    wspec = lambda k, n: pl.BlockSpec((1, k, n), lambda j, be, src, nu: (be[j], 0, 0))
    return pl.pallas_call(
        _expert_kernel,
        grid_spec=pltpu.PrefetchScalarGridSpec(
            num_scalar_prefetch=3,
            grid=(n_slots // tb,),
            in_specs=[pl.BlockSpec((tb, PACK_COLS), lambda j, be, src, nu: (src[j], 0)),
                      wspec(D_MODEL, D_FF), wspec(D_MODEL, D_FF), wspec(D_FF, D_MODEL),
                      wspec(1, D_FF), wspec(1, D_FF), wspec(1, D_MODEL)],
            out_specs=pl.BlockSpec((tb, D_MODEL), lambda j, be, src, nu: (j, 0)),
        ),
        out_shape=jax.ShapeDtypeStruct((n_slots, D_MODEL), F32),
        compiler_params=pltpu.CompilerParams(
            dimension_semantics=("arbitrary",), vmem_limit_bytes=VMEM_LIMIT_BYTES),
        name="experts",
    )(blk_expert, blk_src, n_used, xs, wg, wl, wd, bg, bl, bd)


COMBINE_TILE = 1024


def _combine_kernel(dest_ref, h_ref, gt_ref, g_ref, ys_ref, o_ref, buf, sem):
    tm = h_ref.shape[0]

    def issue(t, carry):
        for k in range(TOP_K):
            d = dest_ref[0, 0, t * TOP_K + k]
            pltpu.make_async_copy(ys_ref.at[pl.ds(d, 1), :], buf.at[k, pl.ds(t, 1), :], sem).start()
        return carry

    lax.fori_loop(0, tm, issue, 0, unroll=8)
    for k in range(TOP_K):
        pltpu.make_async_copy(ys_ref.at[pl.ds(0, tm), :], buf.at[k], sem).wait()
    gates = gt_ref[...]
    out = h_ref[...]
    for k in range(TOP_K):
        out = out + gates[:, k:k + 1] * buf[k]
    o_ref[...] = _rms(out, g_ref[...])


def _combine(dest3d, h, gates, final_g, ys):
    n_tok = h.shape[0]
    tm = COMBINE_TILE
    return pl.pallas_call(
        _combine_kernel,
        grid=(n_tok // tm,),
        in_specs=[pl.BlockSpec((1, 1, tm * TOP_K), lambda i: (i, 0, 0), memory_space=pltpu.SMEM),
                  pl.BlockSpec((tm, D_MODEL), lambda i: (i, 0)),
                  pl.BlockSpec((tm, TOP_K), lambda i: (i, 0)),
                  pl.BlockSpec((1, D_MODEL), lambda i: (0, 0)),
                  pl.BlockSpec(memory_space=pl.ANY)],
        out_specs=pl.BlockSpec((tm, D_MODEL), lambda i: (i, 0)),
        out_shape=jax.ShapeDtypeStruct((n_tok, D_MODEL), F32),
        scratch_shapes=[pltpu.VMEM((TOP_K, tm, D_MODEL), F32), pltpu.SemaphoreType.DMA(())],
        compiler_params=pltpu.CompilerParams(
            dimension_semantics=("arbitrary",), vmem_limit_bytes=VMEM_LIMIT_BYTES),
        name="combine",
    )(dest3d, h, gates, final_g, ys)


def _moe_stage(x2d, yr2d, yd2d, p):
    n_tok = x2d.shape[0]
    row = lambda t: t.reshape(1, -1)
    w_out = p['w_out'][0].astype(BF16)
    rw = p['router_w'][0]
    rw_hi = rw.astype(BF16)
    rw_lo = (rw - rw_hi.astype(F32)).astype(BF16)
    h, xp, top_i, gates, rank, counts = _mix_router(
        x2d, yr2d, yd2d, w_out[:RWKV_WIDTH], w_out[RWKV_WIDTH:], row(p['norm2_g'][0]),
        rw_hi, rw_lo, row(p['router_b'][0]))

    tb = EXPERT_TILE
    n_blocks = (n_tok * TOP_K) // tb + N_EXPERTS
    counts = counts[0].astype(jnp.int32)
    padded = (counts + tb - 1) // tb * tb
    experts = jnp.arange(N_EXPERTS, dtype=jnp.int32)
    pad_end = jnp.sum(jnp.where(experts[None, :] <= experts[:, None], padded[None, :], 0), axis=1)
    pad_start = pad_end - padded
    dest = rank + jnp.sum(jnp.where(top_i[..., None] == experts, pad_start, 0), axis=-1)
    n_used = pad_end[-1] // tb
    blk_src = jnp.minimum(jnp.arange(n_blocks, dtype=jnp.int32), n_used - 1)
    blk_expert = jnp.minimum(
        jnp.sum((pad_end[None, :] <= (blk_src * tb)[:, None]).astype(jnp.int32), axis=1), N_EXPERTS - 1)

    xs_init = jnp.zeros((n_blocks * tb, PACK_COLS), jnp.uint32)
    xs = _dispatch(dest.reshape(n_tok // DISPATCH_TILE, 1, DISPATCH_TILE * TOP_K), xp, xs_init)

    w_up = p['exp_w_up'][0]
    b_up = p['exp_b_up'][0]
    wg, wl = _split_up(w_up)
    bg, bl = b_up[:, None, 0::2], b_up[:, None, 1::2]
    ys = _experts(blk_expert, blk_src.astype(jnp.int32), n_used.reshape(1).astype(jnp.int32), xs, wg, wl,
                  p['exp_w_down'][0].astype(BF16), bg, bl, p['exp_b_down'][0][:, None, :])
    return _combine(dest.reshape(n_tok // COMBINE_TILE, 1, COMBINE_TILE * TOP_K), h, gates,
                    row(p['final_g']), ys)


def _moe_stage_test(h2d, p):
    zeros = jnp.zeros((h2d.shape[0], RWKV_WIDTH), F32)
    return _moe_stage(h2d, zeros, zeros, p)


def kernel(x, norm1_g, w_in, rwkv_mu, rwkv_w0, rwkv_wb, rwkv_a0, rwkv_ab, rwkv_gb, rwkv_kk, rwkv_ka, rwkv_rk, rwkv_ln_g, rwkv_ln_b, diff_lq1, diff_lk1, diff_lq2, diff_lk2, diff_subln_g, w_out, norm2_g, router_w, router_b, exp_w_up, exp_b_up, exp_w_down, exp_b_down, final_g):
    p = dict(rwkv_mu=rwkv_mu, rwkv_w0=rwkv_w0, rwkv_wb=rwkv_wb, rwkv_a0=rwkv_a0, rwkv_ab=rwkv_ab,
             rwkv_gb=rwkv_gb, rwkv_kk=rwkv_kk, rwkv_ka=rwkv_ka, rwkv_rk=rwkv_rk,
             rwkv_ln_g=rwkv_ln_g, rwkv_ln_b=rwkv_ln_b, diff_lq1=diff_lq1, diff_lk1=diff_lk1,
             diff_lq2=diff_lq2, diff_lk2=diff_lk2, diff_subln_g=diff_subln_g, w_out=w_out,
             norm2_g=norm2_g, router_w=router_w, router_b=router_b, exp_w_up=exp_w_up,
             exp_b_up=exp_b_up, exp_w_down=exp_w_down, exp_b_down=exp_b_down, final_g=final_g)
    bsz, seq, d = x.shape
    n_tok = bsz * seq
    x2d = x.reshape(n_tok, d)
    proj = _inproj(x2d, norm1_g, w_in[0].astype(BF16)).reshape(bsz, seq, IN_COLS)
    y_rwkv = _rwkv_stage(proj, p).reshape(n_tok, RWKV_WIDTH)
    y_diff = _diff_stage(proj, p).reshape(n_tok, DIFF_WIDTH)
    return _moe_stage(x2d, y_rwkv, y_diff, p).reshape(bsz, seq, d)
```

```python
import functools
import math

import jax
import jax.numpy as jnp
from jax import lax
from jax.experimental import pallas as pl
from jax.experimental.pallas import tpu as pltpu

F32 = jnp.float32
BF16 = jnp.bfloat16

D_MODEL = 1024
RWKV_WIDTH = 512
RWKV_HEAD = 64
DECAY_LORA = 64
ICL_LORA = 64
GATE_LORA = 128
GN_EPS = 64e-5
DIFF_WIDTH = 512
DIFF_HEAD = 64
DIFF_HEADS = 4
DIFF_VDIM = 128
RWKV_COLS = 3 * RWKV_WIDTH + DECAY_LORA + ICL_LORA + GATE_LORA
IN_COLS = RWKV_COLS + 3 * DIFF_WIDTH
N_EXPERTS = 32
TOP_K = 4
D_FF = 1024
SWIGLU_ALPHA = 1.702
SWIGLU_LIMIT = 7.0
NORM_EPS = 1e-5
LAMBDA_INIT = 0.8 - 0.6 * math.exp(-0.0)

LANES = 128
VMEM_LIMIT_BYTES = 56 * 1024 * 1024

ROW_TILE = 512
RWKV_CHUNK = 64
RWKV_GROUP = 4
GROUP_LANES = RWKV_GROUP * RWKV_HEAD
PREP_UNROLL = 2


def _dot(a, b):
    return jnp.dot(a, b, preferred_element_type=F32)


def _dot_nt(a, b):
    return lax.dot_general(a, b, (((1,), (1,)), ((), ())), preferred_element_type=F32)


def _dot_tn(a, b):
    return lax.dot_general(a, b, (((0,), (0,)), ((), ())), preferred_element_type=F32)


def _split2(x):
    hi = x.astype(BF16)
    lo = (x - hi.astype(F32)).astype(BF16)
    return hi, lo


def _rms(x, g):
    return x * lax.rsqrt(jnp.mean(x * x, axis=-1, keepdims=True) + NORM_EPS) * g


def _inproj_kernel(x_ref, g_ref, w_ref, o_ref):
    xn = _rms(x_ref[...], g_ref[...])
    o_ref[...] = _dot(xn.astype(BF16), w_ref[...])


def _inproj(x2d, g, w_bf16):
    n_tok = x2d.shape[0]
    return pl.pallas_call(
        _inproj_kernel,
        grid=(n_tok // ROW_TILE,),
        in_specs=[
            pl.BlockSpec((ROW_TILE, D_MODEL), lambda i: (i, 0)),
            pl.BlockSpec((1, D_MODEL), lambda i: (0, 0)),
            pl.BlockSpec((D_MODEL, IN_COLS), lambda i: (0, 0)),
        ],
        out_specs=pl.BlockSpec((ROW_TILE, IN_COLS), lambda i: (i, 0)),
        out_shape=jax.ShapeDtypeStruct((n_tok, IN_COLS), F32),
        compiler_params=pltpu.CompilerParams(
            dimension_semantics=("arbitrary",), vmem_limit_bytes=VMEM_LIMIT_BYTES),
        name="inproj",
    )(x2d, g, w_bf16)


def _head_sum(x, bd_ones):
    xb = x.astype(BF16)
    return jnp.concatenate(
        [_dot(xb[:, g * GROUP_LANES:(g + 1) * GROUP_LANES], bd_ones)
         for g in range(RWKV_WIDTH // GROUP_LANES)], axis=1)


def _rwkv_kernel(p_ref, mu_ref, w0_ref, wb_ref, a0_ref, ab_ref, gb_ref, kkw_ref, ka_ref,
                 rk_ref, lng_ref, lnb_ref, o_ref,
                 carry_ref, state_ref, r_s, k_s, v_s, kk_s, b_s, lw_s, y_s,
                 lhs_s, w0_s, yloc_s, arb_s, kb_s, dec_s):
    ts = p_ref.shape[1]
    n_chunks = ts // RWKV_CHUNK
    n_groups = RWKV_WIDTH // GROUP_LANES

    @pl.when(pl.program_id(1) == 0)
    def _():
        carry_ref[...] = jnp.zeros_like(carry_ref)
        state_ref[...] = jnp.zeros_like(state_ref)

    row = lax.broadcasted_iota(jnp.int32, (ts, 1), 0)

    def shifted(c0, c1):
        x = p_ref[0, :, c0:c1]
        prev = pltpu.roll(x, 1, 0)
        prev = jnp.where(row == 0, carry_ref[:, c0:c1], prev)
        return x + (prev - x) * mu_ref[:, c0:c1]

    w1, w2, w3 = RWKV_WIDTH, 2 * RWKV_WIDTH, 3 * RWKV_WIDTH
    r = shifted(0, w1)
    k = shifted(w1, w2)
    v = shifted(w2, w3)
    wa_lo = shifted(w3, w3 + DECAY_LORA + ICL_LORA)
    g_lo = shifted(w3 + DECAY_LORA + ICL_LORA, RWKV_COLS)
    carry_ref[...] = p_ref[0, ts - 1:ts, :]

    gi = lax.broadcasted_iota(jnp.int32, (GROUP_LANES, GROUP_LANES), 0)
    gj = lax.broadcasted_iota(jnp.int32, (GROUP_LANES, GROUP_LANES), 1)
    same_head = (gi // RWKV_HEAD) == (gj // RWKV_HEAD)
    strict = same_head & (gi % RWKV_CHUNK > gj % RWKV_CHUNK)
    incl = same_head & (gi % RWKV_CHUNK >= gj % RWKV_CHUNK)
    eye = (gi == gj).astype(F32)
    bd_ones = same_head.astype(BF16)

    z = w0_ref[...] + _dot(jnp.tanh(wa_lo).astype(BF16), wb_ref[...])
    nz = -z
    softplus = jnp.maximum(nz, 0.0) + jnp.log1p(jnp.exp(-jnp.abs(nz)))
    lw = -jnp.exp(-softplus - 0.5)
    a = jax.nn.sigmoid(a0_ref[...] + _dot(wa_lo.astype(BF16), ab_ref[...]))
    gate = _dot(jax.nn.sigmoid(g_lo).astype(BF16), gb_ref[...])
    kk = k * kkw_ref[...]
    kk = kk * lax.rsqrt(jnp.maximum(_head_sum(kk * kk, bd_ones), 1e-24))
    k2 = k * (1.0 + (a - 1.0) * ka_ref[...])
    r_s[...] = r
    k_s[...] = k2
    v_s[...] = v
    kk_s[...] = kk
    b_s[...] = kk * a
    lw_s[...] = lw

    ci = lax.broadcasted_iota(jnp.int32, (RWKV_CHUNK, RWKV_CHUNK), 0)
    cj = lax.broadcasted_iota(jnp.int32, (RWKV_CHUNK, RWKV_CHUNK), 1)
    tri = (ci >= cj).astype(BF16)
    tri2 = jnp.concatenate([tri, tri], axis=1)

    def tile_heads(x):
        return jnp.concatenate([x] * RWKV_GROUP, axis=0)

    def fold_heads(x):
        out = x[0:RWKV_CHUNK]
        for i in range(1, RWKV_GROUP):
            out = out + x[i * RWKV_CHUNK:(i + 1) * RWKV_CHUNK]
        return out

    def chunk_rows(c):
        return slice(c * RWKV_CHUNK, (c + 1) * RWKV_CHUNK)

    def prepare(cp):
        units = [(cp * PREP_UNROLL + i, g) for i in range(PREP_UNROLL) for g in range(n_groups)]
        lanes = [slice(g * GROUP_LANES, (g + 1) * GROUP_LANES) for _, g in units]
        rows = [chunk_rows(c) for c, _ in units]
        load = lambda ref: [ref[rw, ln] for rw, ln in zip(rows, lanes)]
        rc, kc, vc, kkc, bc, lwc = (load(ref) for ref in (r_s, k_s, v_s, kk_s, b_s, lw_s))
        cum = [_dot(tri2, jnp.concatenate(_split2(x), axis=0)) for x in lwc]
        yield
        cum_end = [x[RWKV_CHUNK - 1:RWKV_CHUNK, :] for x in cum]
        for (c, _), ln, ce in zip(units, lanes, cum_end):
            dec_s[c, :, ln] = jnp.exp(ce)
        e_neg = [jnp.exp(-x) for x in cum]
        e_end = [jnp.exp(ce - x) for ce, x in zip(cum_end, cum)]
        kkd = [a * jnp.exp(x - lw) for a, x, lw in zip(kkc, cum, lwc)]
        rd = [a * jnp.exp(x) for a, x in zip(rc, cum)]
        spread = lambda xs: [jnp.where(same_head, tile_heads(x), 0.0).astype(BF16) for x in xs]
        kkd_s, rd_s, v_st = spread(kkd), spread(rd), spread(vc)
        rhs = [jnp.concatenate([tile_heads(b * e), tile_heads(k * e)], axis=0).astype(BF16)
               for b, k, e in zip(bc, kc, e_neg)]
        yield
        gram = [_dot_nt(jnp.concatenate([a, b], axis=0), w) for a, b, w in zip(kkd_s, rd_s, rhs)]
        yield
        gl = GROUP_LANES
        a_ab = [jnp.where(strict, x[:gl, :gl], 0.0) for x in gram]
        a_rb = [jnp.where(incl, x[gl:, :gl], 0.0).astype(BF16) for x in gram]
        a_ak = [jnp.where(strict, x[:gl, gl:], 0.0).astype(BF16) for x in gram]
        a_rk = [jnp.where(incl, x[gl:, gl:], 0.0).astype(BF16) for x in gram]
        av = [_dot(jnp.concatenate([a, b], axis=0), v) for a, b, v in zip(a_ak, a_rk, v_st)]
        yield

        pw = [x.astype(BF16) for x in a_ab]
        t_inv = [eye - x for x in a_ab]
        for _ in range(int(math.log2(RWKV_CHUNK)) - 1):
            pw = [_dot(x, x).astype(BF16) for x in pw]
            t_inv = [t + _dot(t.astype(BF16), x) for t, x in zip(t_inv, pw)]
            yield
        sol = [_dot(t.astype(BF16), jnp.concatenate([a, x[:gl].astype(BF16)], axis=1))
               for t, a, x in zip(t_inv, kkd_s, av)]
        yield
        for i, (c, g) in enumerate(units):
            lhs_s[c, g] = jnp.concatenate([sol[i][:, :gl].astype(BF16), rd_s[i]], axis=0)
            w0_s[c, g] = sol[i][:, gl:]
            yloc_s[c, g] = fold_heads(av[i][gl:])
            arb_s[c, g] = a_rb[i]
            kb_s[c, g] = jnp.concatenate([kc[i] * e_end[i], bc[i] * e_end[i]], axis=0).astype(BF16)

    def advance(c):
        rows = chunk_rows(c)
        gs = range(n_groups)
        lanes = [slice(g * GROUP_LANES, (g + 1) * GROUP_LANES) for g in gs]
        state = [state_ref[g] for g in gs]
        m1 = [_dot_nt(lhs_s[c, g], state[g].astype(BF16)) for g in gs]
        yield
        sa_st = [m1[g][:GROUP_LANES] + w0_s[c, g] for g in gs]
        y_st = [m1[g][GROUP_LANES:] - _dot(arb_s[c, g], sa_st[g].astype(BF16)) for g in gs]
        yield
        upd = [_dot_tn(jnp.concatenate([v_s[rows, lanes[g]], -fold_heads(sa_st[g])], axis=0).astype(BF16),
                       kb_s[c, g]) for g in gs]
        yield
        for g in gs:
            y_s[rows, lanes[g]] = fold_heads(y_st[g]) + yloc_s[c, g]
            state_ref[g] = state[g] * dec_s[c, :, lanes[g]] + jnp.where(same_head, upd[g], 0.0)
        yield

    def advance_chunks(cp):
        for i in range(PREP_UNROLL):
            yield from advance(cp * PREP_UNROLL + i)

    def interleave(*stages):
        live = list(stages)
        while live:
            for gen in list(live):
                if next(gen, StopIteration) is StopIteration:
                    live.remove(gen)

    n_pairs = n_chunks // PREP_UNROLL
    interleave(prepare(0))
    for cp in range(n_pairs):
        ahead = [prepare(cp + 1)] if cp + 1 < n_pairs else []
        interleave(*ahead, advance_chunks(cp))

    y = y_s[...]
    inv_n = 1.0 / RWKV_HEAD
    mean = _head_sum(y, bd_ones) * inv_n
    d = y - mean
    var = _head_sum(d * d, bd_ones) * inv_n
    yn = d * lax.rsqrt(var + GN_EPS) * lng_ref[...] + lnb_ref[...]
    bonus = _head_sum(r_s[...] * k_s[...] * rk_ref[...], bd_ones) * v_s[...]
    o_ref[0] = (yn + bonus) * gate


def _rwkv(proj3d, mu, w0, wb_pad, a0, ab_pad, gb, kkw, ka, rk, lng, lnb):
    bsz, seq, _ = proj3d.shape
    ts = ROW_TILE
    vec = lambda n: pl.BlockSpec((1, n), lambda b, s: (0, 0))
    mat = lambda m, n: pl.BlockSpec((m, n), lambda b, s: (0, 0))
    tile_f32 = pltpu.VMEM((ts, RWKV_WIDTH), F32)
    n_chunks = ts // RWKV_CHUNK
    n_groups = RWKV_WIDTH // GROUP_LANES
    return pl.pallas_call(
        _rwkv_kernel,
        grid=(bsz, seq // ts),
        in_specs=[
            pl.BlockSpec((1, ts, RWKV_COLS), lambda b, s: (b, s, 0)),
            vec(RWKV_COLS), vec(RWKV_WIDTH), mat(DECAY_LORA + ICL_LORA, RWKV_WIDTH),
            vec(RWKV_WIDTH), mat(DECAY_LORA + ICL_LORA, RWKV_WIDTH), mat(GATE_LORA, RWKV_WIDTH),
            vec(RWKV_WIDTH), vec(RWKV_WIDTH), vec(RWKV_WIDTH), vec(RWKV_WIDTH), vec(RWKV_WIDTH),
        ],
        out_specs=pl.BlockSpec((1, ts, RWKV_WIDTH), lambda b, s: (b, s, 0)),
        out_shape=jax.ShapeDtypeStruct((bsz, seq, RWKV_WIDTH), F32),
        scratch_shapes=[
            pltpu.VMEM((1, RWKV_COLS), F32),
            pltpu.VMEM((n_groups, GROUP_LANES, GROUP_LANES), F32),
            tile_f32, tile_f32, tile_f32, tile_f32, tile_f32, tile_f32, tile_f32,
            pltpu.VMEM((n_chunks, n_groups, 2 * GROUP_LANES, GROUP_LANES), BF16),
            pltpu.VMEM((n_chunks, n_groups, GROUP_LANES, GROUP_LANES), F32),
            pltpu.VMEM((n_chunks, n_groups, RWKV_CHUNK, GROUP_LANES), F32),
            pltpu.VMEM((n_chunks, n_groups, GROUP_LANES, GROUP_LANES), BF16),
            pltpu.VMEM((n_chunks, n_groups, 2 * RWKV_CHUNK, GROUP_LANES), BF16),
            pltpu.VMEM((n_chunks, 1, RWKV_WIDTH), F32),
        ],
        compiler_params=pltpu.CompilerParams(
            dimension_semantics=("arbitrary", "arbitrary"), vmem_limit_bytes=VMEM_LIMIT_BYTES),
        name="rwkv7",
    )(proj3d, mu, w0, wb_pad, a0, ab_pad, gb, kkw, ka, rk, lng, lnb)


def _rwkv_stage(proj3d, p):
    row = lambda t: t.reshape(1, -1)
    zeros = jnp.zeros((DECAY_LORA, RWKV_WIDTH), F32)
    wb_pad = jnp.concatenate([p['rwkv_wb'][0], zeros], axis=0).astype(BF16)
    ab_pad = jnp.concatenate([zeros, p['rwkv_ab'][0]], axis=0).astype(BF16)
    return _rwkv(proj3d, row(p['rwkv_mu'][0]), row(p['rwkv_w0'][0]), wb_pad, row(p['rwkv_a0'][0]), ab_pad,
                 p['rwkv_gb'][0].astype(BF16), row(p['rwkv_kk'][0]), row(p['rwkv_ka'][0]),
                 row(p['rwkv_rk'][0]), row(p['rwkv_ln_g'][0]), row(p['rwkv_ln_b'][0]))


ATT_TILE = 256
ATT_LOCKSTEP = 2
ATT_AUG = 2 * LANES
POS_SPLIT_BITS = 6
MASK_VALUE = -1e30


def _alibi_lanes(n, pos0, slope, key_side):
    pos = pos0 + lax.broadcasted_iota(jnp.int32, (n, LANES), 0)
    lane = lax.broadcasted_iota(jnp.int32, (n, LANES), 1)
    hi = (pos >> POS_SPLIT_BITS).astype(F32) * (slope * float(1 << POS_SPLIT_BITS))
    lo = (pos & ((1 << POS_SPLIT_BITS) - 1)).astype(F32) * slope
    if key_side:
        return jnp.where(lane == 0, hi, jnp.where(lane == 1, lo, jnp.where(lane < 4, 1.0, 0.0)))
    return jnp.where(lane < 2, 1.0, jnp.where(lane == 2, -hi, jnp.where(lane == 3, -lo, 0.0)))


def _diff_kernel(slope_ref, lq1_ref, lk1_ref, lq2_ref, lk2_ref, g_ref, *refs):
    nh = DIFF_HEADS
    q_refs, k_refs, v_refs = refs[:nh], refs[nh:2 * nh], refs[2 * nh:3 * nh]
    o_ref, ks_ref, vs_ref, m_ref, acc_ref = refs[3 * nh:]
    tq = q_refs[0].shape[1]
    seq = k_refs[0].shape[1]
    qi = pl.program_id(1)
    slopes = [slope_ref[h][:, :1] for h in range(nh)]

    @pl.when(qi == 0)
    def _():
        for h in range(nh):
            ks_ref[h, :, :LANES] = k_refs[h][0].astype(BF16)
            ks_ref[h, :, LANES:] = _alibi_lanes(seq, 0, slopes[h], True).astype(BF16)
            vs_ref[h, :, :LANES] = v_refs[h][0].astype(BF16)
            vs_ref[h, :, LANES:] = jnp.ones((seq, LANES), BF16)

    lane = lax.broadcasted_iota(jnp.int32, (1, LANES), 1)
    qa = []
    for h in range(nh):
        q = q_refs[h][0] * (DIFF_HEAD ** -0.5)
        q_terms = _alibi_lanes(tq, qi * tq, slopes[h], False)
        qa.append(jnp.concatenate(
            [jnp.concatenate([jnp.where(lane < DIFF_HEAD, q, 0.0), q_terms], axis=1),
             jnp.concatenate([jnp.where(lane >= DIFF_HEAD, q, 0.0), q_terms], axis=1)],
            axis=0).astype(BF16))
    m_ref[...] = jnp.full_like(m_ref, MASK_VALUE)
    acc_ref[...] = jnp.zeros_like(acc_ref)
    causal = (lax.broadcasted_iota(jnp.int32, (tq, tq), 0)
              >= lax.broadcasted_iota(jnp.int32, (tq, tq), 1))
    causal = jnp.concatenate([causal, causal], axis=0)

    def process(j, diagonal):
        rows = pl.ds(pl.multiple_of(j * tq, tq), tq)
        for h0 in range(0, nh, ATT_LOCKSTEP):
            hs = range(h0, h0 + ATT_LOCKSTEP)
            s = {h: _dot_nt(qa[h], ks_ref[h, rows, :]) for h in hs}
            if diagonal:
                s = {h: jnp.where(causal, s[h], MASK_VALUE) for h in hs}
            m_old = {h: m_ref[h] for h in hs}
            m_new = {h: jnp.maximum(m_old[h], jnp.max(s[h], axis=-1, keepdims=True)) for h in hs}
            pr = {h: jnp.exp(s[h] - jnp.concatenate([m_new[h], m_new[h]], axis=1)).astype(BF16)
                  for h in hs}
            pv = {h: _dot(pr[h], vs_ref[h, rows, :]) for h in hs}
            for h in hs:
                alpha = jnp.exp(m_old[h] - m_new[h])
                acc_ref[h] = jnp.concatenate([alpha, alpha], axis=1) * acc_ref[h] + pv[h]
                m_ref[h] = m_new[h]

    def off_diagonal(j, carry):
        process(j, False)
        return carry

    lax.fori_loop(0, qi, off_diagonal, 0)
    process(qi, True)

    lam = (jnp.exp(jnp.sum(lq1_ref[...] * lk1_ref[...], axis=-1, keepdims=True))
           - jnp.exp(jnp.sum(lq2_ref[...] * lk2_ref[...], axis=-1, keepdims=True)) + LAMBDA_INIT)
    for h in range(nh):
        acc = acc_ref[h]
        o12 = acc[:, :DIFF_VDIM] / acc[:, DIFF_VDIM:]
        o = o12[:tq] - lam * o12[tq:]
        o_ref[0, :, h * DIFF_VDIM:(h + 1) * DIFF_VDIM] = _rms(o, g_ref[...]) * (1.0 - LAMBDA_INIT)


def _diff(proj3d, slopes, lq1, lk1, lq2, lk2, subln_g):
    bsz, seq, _ = proj3d.shape
    tq = ATT_TILE
    q_blk0 = RWKV_COLS // DIFF_VDIM
    k_blk0 = q_blk0 + DIFF_HEADS
    v_blk0 = k_blk0 + DIFF_HEADS
    nh = DIFF_HEADS
    vec = lambda n: pl.BlockSpec((1, n), lambda b, i: (0, 0))
    q_spec = lambda h: pl.BlockSpec((1, tq, DIFF_VDIM), lambda b, i: (b, i, q_blk0 + h))
    kv_spec = lambda blk: pl.BlockSpec((1, seq, DIFF_VDIM), lambda b, i: (b, 0, blk))
    return pl.pallas_call(
        _diff_kernel,
        grid=(bsz, seq // tq),
        in_specs=([pl.BlockSpec((nh, 1, LANES), lambda b, i: (0, 0, 0)),
                   vec(DIFF_HEAD), vec(DIFF_HEAD), vec(DIFF_HEAD), vec(DIFF_HEAD), vec(DIFF_VDIM)]
                  + [q_spec(h) for h in range(nh)]
                  + [kv_spec(k_blk0 + h) for h in range(nh)]
                  + [kv_spec(v_blk0 + h) for h in range(nh)]),
        out_specs=pl.BlockSpec((1, tq, DIFF_WIDTH), lambda b, i: (b, i, 0)),
        out_shape=jax.ShapeDtypeStruct((bsz, seq, DIFF_WIDTH), F32),
        scratch_shapes=[pltpu.VMEM((nh, seq, ATT_AUG), BF16), pltpu.VMEM((nh, seq, ATT_AUG), BF16),
                        pltpu.VMEM((nh, 2 * tq, LANES), F32), pltpu.VMEM((nh, 2 * tq, ATT_AUG), F32)],
        compiler_params=pltpu.CompilerParams(
            dimension_semantics=("arbitrary", "arbitrary"), vmem_limit_bytes=VMEM_LIMIT_BYTES),
        name="diffattn",
    )(slopes, lq1, lk1, lq2, lk2, subln_g, *([proj3d] * (3 * nh)))


def _diff_stage(proj3d, p):
    row = lambda t: t.reshape(1, -1)
    slopes = 2.0 ** (-8.0 * (jnp.arange(DIFF_HEADS, dtype=F32) + 1.0) / DIFF_HEADS)
    slopes = jnp.broadcast_to(slopes[:, None, None], (DIFF_HEADS, 1, LANES))
    return _diff(proj3d, slopes, row(p['diff_lq1'][0]), row(p['diff_lk1'][0]), row(p['diff_lq2'][0]),
                 row(p['diff_lk2'][0]), row(p['diff_subln_g'][0]))


PACK_COLS = D_MODEL // 2
PACK_CHUNKS = PACK_COLS // LANES
OUT_CHUNKS = D_MODEL // LANES


def _pack_bf16_pairs(x):
    xb = x.astype(BF16).astype(F32)
    lo = lax.bitcast_convert_type(xb[:, :PACK_COLS], jnp.uint32) >> 16
    hi = lax.bitcast_convert_type(xb[:, PACK_COLS:], jnp.uint32)
    return lo | hi


def _unpack_bf16_pairs(xp):
    lo = lax.bitcast_convert_type(xp << 16, F32)
    hi = lax.bitcast_convert_type(xp & jnp.uint32(0xFFFF0000), F32)
    return jnp.concatenate([lo, hi], axis=1).astype(BF16)


def _store_row_chunks(ref, value):
    n_rows, width = value.shape
    n_chunks = width // LANES
    for c in range(n_chunks):
        ref[pl.ds(c, n_rows, stride=n_chunks), :] = value[:, c * LANES:(c + 1) * LANES]


def _load_row_chunks(ref, n_rows, n_chunks, first_row=0):
    return jnp.concatenate(
        [ref[pl.ds(first_row * n_chunks + c, n_rows, stride=n_chunks), :] for c in range(n_chunks)], axis=1)


def _row_chunks(ref, row, n_chunks):
    return ref.at[pl.ds(pl.multiple_of(row * n_chunks, n_chunks), n_chunks), :]


def _mix_router_kernel(x_ref, yr_ref, yd_ref, wt_ref, wb_ref, g_ref, rwh_ref, rwl_ref, rb_ref,
                       h_ref, xp_ref, ti_ref, gt_ref, rk_ref, cnt_ref):
    tm = x_ref.shape[0]

    @pl.when(pl.program_id(0) == 0)
    def _():
        cnt_ref[...] = jnp.zeros_like(cnt_ref)

    h = (x_ref[...] + _dot(yr_ref[...].astype(BF16), wt_ref[...])
         + _dot(yd_ref[...].astype(BF16), wb_ref[...]))
    h_ref[...] = h
    xn = _rms(h, g_ref[...])
    _store_row_chunks(xp_ref, _pack_bf16_pairs(xn))
    x_hi, x_lo = _split2(xn)
    logits = (_dot(x_hi, rwh_ref[...]) + _dot(x_hi, rwl_ref[...]) + _dot(x_lo, rwh_ref[...])
              + rb_ref[...])

    lane_e = lax.broadcasted_iota(jnp.int32, (tm, N_EXPERTS), 1)
    lane_k = lax.broadcasted_iota(jnp.int32, (tm, TOP_K), 1)
    vals = logits
    tops, sels = [], []
    top_i = jnp.zeros((tm, TOP_K), jnp.int32)
    for k in range(TOP_K):
        m = jnp.max(vals, axis=-1, keepdims=True)
        idx = jnp.min(jnp.where(vals == m, lane_e, N_EXPERTS), axis=-1, keepdims=True)
        sel = lane_e == idx
        vals = jnp.where(sel, -jnp.inf, vals)
        tops.append(m)
        sels.append(sel)
        top_i = jnp.where(lane_k == k, idx, top_i)
    ti_ref[...] = top_i

    exps = [jnp.exp(m - tops[0]) for m in tops]
    den = exps[0] + exps[1] + exps[2] + exps[3]
    gates = jnp.zeros((tm, TOP_K), F32)
    for k in range(TOP_K):
        gates = jnp.where(lane_k == k, exps[k] / den, gates)
    gt_ref[...] = gates

    cnt = jnp.zeros((tm, N_EXPERTS), F32)
    for sel in sels:
        cnt = cnt + sel.astype(F32)
    ri = lax.broadcasted_iota(jnp.int32, (tm, tm), 0)
    rj = lax.broadcasted_iota(jnp.int32, (tm, tm), 1)
    before = (rj < ri).astype(BF16)
    prefix = _dot(before, cnt.astype(BF16)) + cnt_ref[...]
    rank = jnp.zeros((tm, TOP_K), F32)
    for k in range(TOP_K):
        rk = jnp.sum(jnp.where(sels[k], prefix, 0.0), axis=-1, keepdims=True)
        rank = jnp.where(lane_k == k, rk, rank)
    rk_ref[...] = rank.astype(jnp.int32)
    cnt_ref[...] += jnp.sum(cnt, axis=0, keepdims=True)


def _mix_router(x2d, yr2d, yd2d, w_top, w_bot, g2, rw_hi, rw_lo, rb):
    n_tok = x2d.shape[0]
    tm = ROW_TILE
    rows = lambda n: pl.BlockSpec((tm, n), lambda i: (i, 0))
    full = lambda m, n: pl.BlockSpec((m, n), lambda i: (0, 0))
    return pl.pallas_call(
        _mix_router_kernel,
        grid=(n_tok // tm,),
        in_specs=[rows(D_MODEL), rows(RWKV_WIDTH), rows(DIFF_WIDTH),
                  full(RWKV_WIDTH, D_MODEL), full(DIFF_WIDTH, D_MODEL), full(1, D_MODEL),
                  full(D_MODEL, N_EXPERTS), full(D_MODEL, N_EXPERTS), full(1, N_EXPERTS)],
        out_specs=[rows(D_MODEL), pl.BlockSpec((tm * PACK_CHUNKS, LANES), lambda i: (i, 0)),
                   rows(TOP_K), rows(TOP_K), rows(TOP_K),
                   full(1, N_EXPERTS)],
        out_shape=[jax.ShapeDtypeStruct((n_tok, D_MODEL), F32),
                   jax.ShapeDtypeStruct((n_tok * PACK_CHUNKS, LANES), jnp.uint32),
                   jax.ShapeDtypeStruct((n_tok, TOP_K), jnp.int32),
                   jax.ShapeDtypeStruct((n_tok, TOP_K), F32),
                   jax.ShapeDtypeStruct((n_tok, TOP_K), jnp.int32),
                   jax.ShapeDtypeStruct((1, N_EXPERTS), F32)],
        compiler_params=pltpu.CompilerParams(
            dimension_semantics=("arbitrary",), vmem_limit_bytes=VMEM_LIMIT_BYTES),
        name="mix_router",
    )(x2d, yr2d, yd2d, w_top, w_bot, g2, rw_hi, rw_lo, rb)


EXPERT_TILE = 512
DISPATCH_TILE = 2048


def _dispatch_kernel(dest_ref, xp_ref, init_ref, xs_ref, sem):
    del init_ref
    tm = xp_ref.shape[0] // PACK_CHUNKS

    def issue(t, carry):
        src = _row_chunks(xp_ref, t, PACK_CHUNKS)
        for k in range(TOP_K):
            d = dest_ref[0, 0, t * TOP_K + k]
            pltpu.make_async_copy(src, _row_chunks(xs_ref, d, PACK_CHUNKS), sem).start(priority=k % 2)
        return carry

    lax.fori_loop(0, tm, issue, 0, unroll=8)
    for k in range(TOP_K):
        pltpu.make_async_copy(xp_ref, xs_ref.at[pl.ds(0, tm * PACK_CHUNKS), :], sem).wait()


def _dispatch(dest3d, xp, xs_init):
    n_tok = xp.shape[0] // PACK_CHUNKS
    tm = DISPATCH_TILE
    return pl.pallas_call(
        _dispatch_kernel,
        grid=(n_tok // tm,),
        in_specs=[pl.BlockSpec((1, 1, tm * TOP_K), lambda i: (i, 0, 0), memory_space=pltpu.SMEM),
                  pl.BlockSpec((tm * PACK_CHUNKS, LANES), lambda i: (i, 0)),
                  pl.BlockSpec(memory_space=pl.ANY)],
        out_specs=pl.BlockSpec(memory_space=pl.ANY),
        out_shape=jax.ShapeDtypeStruct(xs_init.shape, xs_init.dtype),
        scratch_shapes=[pltpu.SemaphoreType.DMA(())],
        input_output_aliases={2: 0},
        compiler_params=pltpu.CompilerParams(
            dimension_semantics=("arbitrary",), vmem_limit_bytes=VMEM_LIMIT_BYTES),
        name="dispatch",
    )(dest3d, xp, xs_init)


SPLIT_BLOCK = 2 * LANES
UP_COLS_PER_STEP = 1024


def _split_up_kernel(w_ref, wg_ref, wl_ref):
    src = lax.broadcasted_iota(jnp.int32, (SPLIT_BLOCK, SPLIT_BLOCK), 0)
    dst = lax.broadcasted_iota(jnp.int32, (SPLIT_BLOCK, SPLIT_BLOCK), 1)
    perm = (src == jnp.where(dst < LANES, 2 * dst, 2 * (dst - LANES) + 1)).astype(BF16)
    for c in range(UP_COLS_PER_STEP // SPLIT_BLOCK):
        blk = w_ref[0, :, c * SPLIT_BLOCK:(c + 1) * SPLIT_BLOCK].astype(BF16)
        out = _dot(blk, perm).astype(BF16)
        wg_ref[0, :, c * LANES:(c + 1) * LANES] = out[:, :LANES]
        wl_ref[0, :, c * LANES:(c + 1) * LANES] = out[:, LANES:]


def _split_up(w_up):
    n_exp, d_in, d_out2 = w_up.shape
    half = UP_COLS_PER_STEP // 2
    out = jax.ShapeDtypeStruct((n_exp, d_in, d_out2 // 2), BF16)
    return pl.pallas_call(
        _split_up_kernel,
        grid=(n_exp, d_out2 // UP_COLS_PER_STEP),
        in_specs=[pl.BlockSpec((1, d_in, UP_COLS_PER_STEP), lambda e, c: (e, 0, c))],
        out_specs=[pl.BlockSpec((1, d_in, half), lambda e, c: (e, 0, c)),
                   pl.BlockSpec((1, d_in, half), lambda e, c: (e, 0, c))],
        out_shape=[out, out],
        compiler_params=pltpu.CompilerParams(
            dimension_semantics=("arbitrary", "arbitrary"), vmem_limit_bytes=VMEM_LIMIT_BYTES),
        name="split_up",
    )(w_up)


def _expert_kernel(be_ref, src_ref, nu_ref, xs_ref, wg_ref, wl_ref, wd_ref, bg_ref, bl_ref, bd_ref, y_ref):
    del be_ref, src_ref
    used = pl.program_id(0) < nu_ref[0]

    @pl.when(jnp.logical_not(used))
    def _():
        y_ref[...] = jnp.zeros_like(y_ref)

    @pl.when(used)
    def _():
        x = _unpack_bf16_pairs(_load_row_chunks(xs_ref, EXPERT_TILE, PACK_CHUNKS))
        glu = jnp.minimum(_dot(x, wg_ref[0]) + bg_ref[0], SWIGLU_LIMIT)
        lin = jnp.clip(_dot(x, wl_ref[0]) + bl_ref[0], -SWIGLU_LIMIT, SWIGLU_LIMIT)
        act = glu * jax.nn.sigmoid(SWIGLU_ALPHA * glu) * (lin + 1.0)
        _store_row_chunks(y_ref, _dot(act.astype(BF16), wd_ref[0]) + bd_ref[0])


def _experts(blk_expert, blk_src, n_used, xs, wg, wl, wd, bg, bl, bd):
    n_slots = xs.shape[0] // PACK_CHUNKS
    tb = EXPERT_TILE
    wspec = lambda k, n: pl.BlockSpec((1, k, n), lambda j, be, src, nu: (be[j], 0, 0))
    return pl.pallas_call(
        _expert_kernel,
        grid_spec=pltpu.PrefetchScalarGridSpec(
            num_scalar_prefetch=3,
            grid=(n_slots // tb,),
            in_specs=[pl.BlockSpec((tb * PACK_CHUNKS, LANES), lambda j, be, src, nu: (src[j], 0)),
                      wspec(D_MODEL, D_FF), wspec(D_MODEL, D_FF), wspec(D_FF, D_MODEL),
                      wspec(1, D_FF), wspec(1, D_FF), wspec(1, D_MODEL)],
            out_specs=pl.BlockSpec((tb * OUT_CHUNKS, LANES), lambda j, be, src, nu: (j, 0)),
        ),
        out_shape=jax.ShapeDtypeStruct((n_slots * OUT_CHUNKS, LANES), F32),
        compiler_params=pltpu.CompilerParams(
            dimension_semantics=("arbitrary",), vmem_limit_bytes=VMEM_LIMIT_BYTES),
        name="experts",
    )(blk_expert, blk_src, n_used, xs, wg, wl, wd, bg, bl, bd)


COMBINE_TILE = 1024


def _combine_kernel(dest_ref, h_ref, gt_ref, g_ref, ys_ref, o_ref, buf, sem):
    tm = h_ref.shape[0]

    def issue(t, carry):
        for k in range(TOP_K):
            d = dest_ref[0, 0, t * TOP_K + k]
            pltpu.make_async_copy(_row_chunks(ys_ref, d, OUT_CHUNKS),
                                  _row_chunks(buf, k * tm + t, OUT_CHUNKS), sem).start(priority=k % 2)
        return carry

    lax.fori_loop(0, tm, issue, 0, unroll=8)
    rows_per_k = tm * OUT_CHUNKS
    for k in range(TOP_K):
        pltpu.make_async_copy(ys_ref.at[pl.ds(0, rows_per_k), :],
                              buf.at[pl.ds(k * rows_per_k, rows_per_k), :], sem).wait()
    gates = gt_ref[...]
    out = h_ref[...]
    for k in range(TOP_K):
        out = out + gates[:, k:k + 1] * _load_row_chunks(buf, tm, OUT_CHUNKS, first_row=k * tm)
    o_ref[...] = _rms(out, g_ref[...])


def _combine(dest3d, h, gates, final_g, ys):
    n_tok = h.shape[0]
    tm = COMBINE_TILE
    return pl.pallas_call(
        _combine_kernel,
        grid=(n_tok // tm,),
        in_specs=[pl.BlockSpec((1, 1, tm * TOP_K), lambda i: (i, 0, 0), memory_space=pltpu.SMEM),
                  pl.BlockSpec((tm, D_MODEL), lambda i: (i, 0)),
                  pl.BlockSpec((tm, TOP_K), lambda i: (i, 0)),
                  pl.BlockSpec((1, D_MODEL), lambda i: (0, 0)),
                  pl.BlockSpec(memory_space=pl.ANY)],
        out_specs=pl.BlockSpec((tm, D_MODEL), lambda i: (i, 0)),
        out_shape=jax.ShapeDtypeStruct((n_tok, D_MODEL), F32),
        scratch_shapes=[pltpu.VMEM((TOP_K * tm * OUT_CHUNKS, LANES), F32), pltpu.SemaphoreType.DMA(())],
        compiler_params=pltpu.CompilerParams(
            dimension_semantics=("arbitrary",), vmem_limit_bytes=VMEM_LIMIT_BYTES),
        name="combine",
    )(dest3d, h, gates, final_g, ys)


def _moe_stage(x2d, yr2d, yd2d, p):
    n_tok = x2d.shape[0]
    row = lambda t: t.reshape(1, -1)
    w_out = p['w_out'][0].astype(BF16)
    rw = p['router_w'][0]
    rw_hi = rw.astype(BF16)
    rw_lo = (rw - rw_hi.astype(F32)).astype(BF16)
    h, xp, top_i, gates, rank, counts = _mix_router(
        x2d, yr2d, yd2d, w_out[:RWKV_WIDTH], w_out[RWKV_WIDTH:], row(p['norm2_g'][0]),
        rw_hi, rw_lo, row(p['router_b'][0]))

    tb = EXPERT_TILE
    n_blocks = (n_tok * TOP_K) // tb + N_EXPERTS
    counts = counts[0].astype(jnp.int32)
    padded = (counts + tb - 1) // tb * tb
    experts = jnp.arange(N_EXPERTS, dtype=jnp.int32)
    pad_end = jnp.sum(jnp.where(experts[None, :] <= experts[:, None], padded[None, :], 0), axis=1)
    pad_start = pad_end - padded
    dest = rank + jnp.sum(jnp.where(top_i[..., None] == experts, pad_start, 0), axis=-1)
    n_used = pad_end[-1] // tb
    blk_src = jnp.minimum(jnp.arange(n_blocks, dtype=jnp.int32), n_used - 1)
    blk_expert = jnp.minimum(
        jnp.sum((pad_end[None, :] <= (blk_src * tb)[:, None]).astype(jnp.int32), axis=1), N_EXPERTS - 1)

    xs_init = jnp.zeros((n_blocks * tb * PACK_CHUNKS, LANES), jnp.uint32)
    xs = _dispatch(dest.reshape(n_tok // DISPATCH_TILE, 1, DISPATCH_TILE * TOP_K), xp, xs_init)

    w_up = p['exp_w_up'][0]
    b_up = p['exp_b_up'][0]
    wg, wl = _split_up(w_up)
    bg, bl = b_up[:, None, 0::2], b_up[:, None, 1::2]
    ys = _experts(blk_expert, blk_src.astype(jnp.int32), n_used.reshape(1).astype(jnp.int32), xs, wg, wl,
                  p['exp_w_down'][0].astype(BF16), bg, bl, p['exp_b_down'][0][:, None, :])
    return _combine(dest.reshape(n_tok // COMBINE_TILE, 1, COMBINE_TILE * TOP_K), h, gates,
                    row(p['final_g']), ys)


def _moe_stage_test(h2d, p):
    zeros = jnp.zeros((h2d.shape[0], RWKV_WIDTH), F32)
    return _moe_stage(h2d, zeros, zeros, p)


def kernel(x, norm1_g, w_in, rwkv_mu, rwkv_w0, rwkv_wb, rwkv_a0, rwkv_ab, rwkv_gb, rwkv_kk, rwkv_ka, rwkv_rk, rwkv_ln_g, rwkv_ln_b, diff_lq1, diff_lk1, diff_lq2, diff_lk2, diff_subln_g, w_out, norm2_g, router_w, router_b, exp_w_up, exp_b_up, exp_w_down, exp_b_down, final_g):
    p = dict(rwkv_mu=rwkv_mu, rwkv_w0=rwkv_w0, rwkv_wb=rwkv_wb, rwkv_a0=rwkv_a0, rwkv_ab=rwkv_ab,
             rwkv_gb=rwkv_gb, rwkv_kk=rwkv_kk, rwkv_ka=rwkv_ka, rwkv_rk=rwkv_rk,
             rwkv_ln_g=rwkv_ln_g, rwkv_ln_b=rwkv_ln_b, diff_lq1=diff_lq1, diff_lk1=diff_lk1,
             diff_lq2=diff_lq2, diff_lk2=diff_lk2, diff_subln_g=diff_subln_g, w_out=w_out,
             norm2_g=norm2_g, router_w=router_w, router_b=router_b, exp_w_up=exp_w_up,
             exp_b_up=exp_b_up, exp_w_down=exp_w_down, exp_b_down=exp_b_down, final_g=final_g)
    bsz, seq, d = x.shape
    n_tok = bsz * seq
    x2d = x.reshape(n_tok, d)
    proj = _inproj(x2d, norm1_g, w_in[0].astype(BF16)).reshape(bsz, seq, IN_COLS)
    y_rwkv = _rwkv_stage(proj, p).reshape(n_tok, RWKV_WIDTH)
    y_diff = _diff_stage(proj, p).reshape(n_tok, DIFF_WIDTH)
    return _moe_stage(x2d, y_rwkv, y_diff, p).reshape(bsz, seq, d)
```

```python
import functools
import math

import jax
import jax.numpy as jnp
from jax import lax
from jax.experimental import pallas as pl
from jax.experimental.pallas import tpu as pltpu

F32 = jnp.float32
BF16 = jnp.bfloat16

D_MODEL = 1024
RWKV_WIDTH = 512
RWKV_HEAD = 64
DECAY_LORA = 64
ICL_LORA = 64
GATE_LORA = 128
GN_EPS = 64e-5
DIFF_WIDTH = 512
DIFF_HEAD = 64
DIFF_HEADS = 4
DIFF_VDIM = 128
RWKV_COLS = 3 * RWKV_WIDTH + DECAY_LORA + ICL_LORA + GATE_LORA
IN_COLS = RWKV_COLS + 3 * DIFF_WIDTH
N_EXPERTS = 32
TOP_K = 4
D_FF = 1024
SWIGLU_ALPHA = 1.702
SWIGLU_LIMIT = 7.0
NORM_EPS = 1e-5
LAMBDA_INIT = 0.8 - 0.6 * math.exp(-0.0)

LANES = 128
VMEM_LIMIT_BYTES = 56 * 1024 * 1024

ROW_TILE = 512
RWKV_CHUNK = 64
RWKV_GROUP = 4
GROUP_LANES = RWKV_GROUP * RWKV_HEAD
PREP_UNROLL = 2


def _dot(a, b):
    return jnp.dot(a, b, preferred_element_type=F32)


def _dot_nt(a, b):
    return lax.dot_general(a, b, (((1,), (1,)), ((), ())), preferred_element_type=F32)


def _dot_tn(a, b):
    return lax.dot_general(a, b, (((0,), (0,)), ((), ())), preferred_element_type=F32)


def _split2(x):
    hi = x.astype(BF16)
    lo = (x - hi.astype(F32)).astype(BF16)
    return hi, lo


def _rms(x, g):
    return x * lax.rsqrt(jnp.mean(x * x, axis=-1, keepdims=True) + NORM_EPS) * g


def _inproj_kernel(x_ref, g_ref, w_ref, o_ref):
    xn = _rms(x_ref[...], g_ref[...])
    o_ref[...] = _dot(xn.astype(BF16), w_ref[...])


def _inproj(x2d, g, w_bf16):
    n_tok = x2d.shape[0]
    return pl.pallas_call(
        _inproj_kernel,
        grid=(n_tok // ROW_TILE,),
        in_specs=[
            pl.BlockSpec((ROW_TILE, D_MODEL), lambda i: (i, 0)),
            pl.BlockSpec((1, D_MODEL), lambda i: (0, 0)),
            pl.BlockSpec((D_MODEL, IN_COLS), lambda i: (0, 0)),
        ],
        out_specs=pl.BlockSpec((ROW_TILE, IN_COLS), lambda i: (i, 0)),
        out_shape=jax.ShapeDtypeStruct((n_tok, IN_COLS), F32),
        compiler_params=pltpu.CompilerParams(
            dimension_semantics=("arbitrary",), vmem_limit_bytes=VMEM_LIMIT_BYTES),
        name="inproj",
    )(x2d, g, w_bf16)


def _head_sum(x, bd_ones):
    xb = x.astype(BF16)
    return jnp.concatenate(
        [_dot(xb[:, g * GROUP_LANES:(g + 1) * GROUP_LANES], bd_ones)
         for g in range(RWKV_WIDTH // GROUP_LANES)], axis=1)


def _rwkv_kernel(p_ref, mu_ref, w0_ref, wb_ref, a0_ref, ab_ref, gb_ref, kkw_ref, ka_ref,
                 rk_ref, lng_ref, lnb_ref, o_ref,
                 carry_ref, state_ref, r_s, k_s, v_s, kk_s, b_s, lw_s, y_s, gate_s,
                 lhs_s, w0_s, yloc_s, arb_s, kb_s, dec_s):
    ts = p_ref.shape[1]
    n_chunks = ts // RWKV_CHUNK
    n_groups = RWKV_WIDTH // GROUP_LANES

    @pl.when(pl.program_id(1) == 0)
    def _():
        carry_ref[...] = jnp.zeros_like(carry_ref)
        state_ref[...] = jnp.zeros_like(state_ref)

    gi = lax.broadcasted_iota(jnp.int32, (GROUP_LANES, GROUP_LANES), 0)
    gj = lax.broadcasted_iota(jnp.int32, (GROUP_LANES, GROUP_LANES), 1)
    same_head = (gi // RWKV_HEAD) == (gj // RWKV_HEAD)
    strict = same_head & (gi % RWKV_CHUNK > gj % RWKV_CHUNK)
    incl = same_head & (gi % RWKV_CHUNK >= gj % RWKV_CHUNK)
    eye = (gi == gj).astype(F32)
    bd_ones = same_head.astype(BF16)

    pair_rows = PREP_UNROLL * RWKV_CHUNK
    first_row = lax.broadcasted_iota(jnp.int32, (pair_rows, 1), 0) == 0
    w1, w2, w3 = RWKV_WIDTH, 2 * RWKV_WIDTH, 3 * RWKV_WIDTH

    def project(cp):
        r0 = cp * pair_rows
        rows = slice(r0, r0 + pair_rows)

        def shifted(c0, c1):
            x = p_ref[0, rows, c0:c1]
            before = carry_ref[:, c0:c1] if cp == 0 else p_ref[0, r0 - 1:r0, c0:c1]
            prev = jnp.where(first_row, before, pltpu.roll(x, 1, 0))
            return x + (prev - x) * mu_ref[:, c0:c1]

        r = shifted(0, w1)
        k = shifted(w1, w2)
        v = shifted(w2, w3)
        yield
        wa_lo = shifted(w3, w3 + DECAY_LORA + ICL_LORA)
        g_lo = shifted(w3 + DECAY_LORA + ICL_LORA, RWKV_COLS)
        z = w0_ref[...] + _dot(jnp.tanh(wa_lo).astype(BF16), wb_ref[...])
        nz = -z
        softplus = jnp.maximum(nz, 0.0) + jnp.log1p(jnp.exp(-jnp.abs(nz)))
        lw_s[rows] = -jnp.exp(-softplus - 0.5)
        yield
        a = jax.nn.sigmoid(a0_ref[...] + _dot(wa_lo.astype(BF16), ab_ref[...]))
        gate_s[rows] = _dot(jax.nn.sigmoid(g_lo).astype(BF16), gb_ref[...])
        yield
        kk = k * kkw_ref[...]
        kk = kk * lax.rsqrt(jnp.maximum(_head_sum(kk * kk, bd_ones), 1e-24))
        r_s[rows] = r
        k_s[rows] = k * (1.0 + (a - 1.0) * ka_ref[...])
        v_s[rows] = v
        kk_s[rows] = kk
        b_s[rows] = kk * a

    def finish(cp):
        rows = slice(cp * pair_rows, (cp + 1) * pair_rows)
        y = y_s[rows]
        inv_n = 1.0 / RWKV_HEAD
        d = y - _head_sum(y, bd_ones) * inv_n
        yield
        var = _head_sum(d * d, bd_ones) * inv_n
        yn = d * lax.rsqrt(var + GN_EPS) * lng_ref[...] + lnb_ref[...]
        yield
        bonus = _head_sum(r_s[rows] * k_s[rows] * rk_ref[...], bd_ones) * v_s[rows]
        o_ref[0, rows] = (yn + bonus) * gate_s[rows]

    ci = lax.broadcasted_iota(jnp.int32, (RWKV_CHUNK, RWKV_CHUNK), 0)
    cj = lax.broadcasted_iota(jnp.int32, (RWKV_CHUNK, RWKV_CHUNK), 1)
    tri = (ci >= cj).astype(BF16)
    tri2 = jnp.concatenate([tri, tri], axis=1)

    def tile_heads(x):
        return jnp.concatenate([x] * RWKV_GROUP, axis=0)

    def fold_heads(x):
        out = x[0:RWKV_CHUNK]
        for i in range(1, RWKV_GROUP):
            out = out + x[i * RWKV_CHUNK:(i + 1) * RWKV_CHUNK]
        return out

    def chunk_rows(c):
        return slice(c * RWKV_CHUNK, (c + 1) * RWKV_CHUNK)

    def prepare(cp):
        units = [(cp * PREP_UNROLL + i, g) for i in range(PREP_UNROLL) for g in range(n_groups)]
        lanes = [slice(g * GROUP_LANES, (g + 1) * GROUP_LANES) for _, g in units]
        rows = [chunk_rows(c) for c, _ in units]
        load = lambda ref: [ref[rw, ln] for rw, ln in zip(rows, lanes)]
        rc, kc, vc, kkc, bc, lwc = (load(ref) for ref in (r_s, k_s, v_s, kk_s, b_s, lw_s))
        cum = [_dot(tri2, jnp.concatenate(_split2(x), axis=0)) for x in lwc]
        yield
        cum_end = [x[RWKV_CHUNK - 1:RWKV_CHUNK, :] for x in cum]
        for (c, _), ln, ce in zip(units, lanes, cum_end):
            dec_s[c, :, ln] = jnp.exp(ce)
        e_neg = [jnp.exp(-x) for x in cum]
        e_end = [jnp.exp(ce - x) for ce, x in zip(cum_end, cum)]
        kkd = [a * jnp.exp(x - lw) for a, x, lw in zip(kkc, cum, lwc)]
        rd = [a * jnp.exp(x) for a, x in zip(rc, cum)]
        spread = lambda xs: [jnp.where(same_head, tile_heads(x), 0.0).astype(BF16) for x in xs]
        kkd_s, rd_s, v_st = spread(kkd), spread(rd), spread(vc)
        rhs = [jnp.concatenate([tile_heads(b * e), tile_heads(k * e)], axis=0).astype(BF16)
               for b, k, e in zip(bc, kc, e_neg)]
        yield
        gram = [_dot_nt(jnp.concatenate([a, b], axis=0), w) for a, b, w in zip(kkd_s, rd_s, rhs)]
        yield
        gl = GROUP_LANES
        a_ab = [jnp.where(strict, x[:gl, :gl], 0.0) for x in gram]
        a_rb = [jnp.where(incl, x[gl:, :gl], 0.0).astype(BF16) for x in gram]
        a_ak = [jnp.where(strict, x[:gl, gl:], 0.0).astype(BF16) for x in gram]
        a_rk = [jnp.where(incl, x[gl:, gl:], 0.0).astype(BF16) for x in gram]
        av = [_dot(jnp.concatenate([a, b], axis=0), v) for a, b, v in zip(a_ak, a_rk, v_st)]
        yield

        pw = [x.astype(BF16) for x in a_ab]
        t_inv = [eye - x for x in a_ab]
        for _ in range(int(math.log2(RWKV_CHUNK)) - 1):
            pw = [_dot(x, x).astype(BF16) for x in pw]
            t_inv = [t + _dot(t.astype(BF16), x) for t, x in zip(t_inv, pw)]
            yield
        sol = [_dot(t.astype(BF16), jnp.concatenate([a, x[:gl].astype(BF16)], axis=1))
               for t, a, x in zip(t_inv, kkd_s, av)]
        yield
        for i, (c, g) in enumerate(units):
            lhs_s[c, g] = jnp.concatenate([sol[i][:, :gl].astype(BF16), rd_s[i]], axis=0)
            w0_s[c, g] = sol[i][:, gl:]
            yloc_s[c, g] = fold_heads(av[i][gl:])
            arb_s[c, g] = a_rb[i]
            kb_s[c, g] = jnp.concatenate([kc[i] * e_end[i], bc[i] * e_end[i]], axis=0).astype(BF16)

    def advance(c):
        rows = chunk_rows(c)
        gs = range(n_groups)
        lanes = [slice(g * GROUP_LANES, (g + 1) * GROUP_LANES) for g in gs]
        state = [state_ref[g] for g in gs]
        m1 = [_dot_nt(lhs_s[c, g], state[g].astype(BF16)) for g in gs]
        yield
        sa_st = [m1[g][:GROUP_LANES] + w0_s[c, g] for g in gs]
        y_st = [m1[g][GROUP_LANES:] - _dot(arb_s[c, g], sa_st[g].astype(BF16)) for g in gs]
        yield
        upd = [_dot_tn(jnp.concatenate([v_s[rows, lanes[g]], -fold_heads(sa_st[g])], axis=0).astype(BF16),
                       kb_s[c, g]) for g in gs]
        yield
        for g in gs:
            y_s[rows, lanes[g]] = fold_heads(y_st[g]) + yloc_s[c, g]
            state_ref[g] = state[g] * dec_s[c, :, lanes[g]] + jnp.where(same_head, upd[g], 0.0)
        yield

    def advance_chunks(cp):
        for i in range(PREP_UNROLL):
            yield from advance(cp * PREP_UNROLL + i)

    def interleave(*stages):
        live = list(stages)
        while live:
            for gen in list(live):
                if next(gen, StopIteration) is StopIteration:
                    live.remove(gen)

    n_pairs = n_chunks // PREP_UNROLL
    interleave(project(0))
    interleave(prepare(0), *([project(1)] if n_pairs > 1 else []))
    for cp in range(n_pairs):
        stages = [advance_chunks(cp)]
        if cp + 1 < n_pairs:
            stages.append(prepare(cp + 1))
        if cp + 2 < n_pairs:
            stages.append(project(cp + 2))
        if cp >= 1:
            stages.append(finish(cp - 1))
        interleave(*stages)
    interleave(finish(n_pairs - 1))
    carry_ref[...] = p_ref[0, ts - 1:ts, :]


def _rwkv(proj3d, mu, w0, wb_pad, a0, ab_pad, gb, kkw, ka, rk, lng, lnb):
    bsz, seq, _ = proj3d.shape
    ts = ROW_TILE
    vec = lambda n: pl.BlockSpec((1, n), lambda b, s: (0, 0))
    mat = lambda m, n: pl.BlockSpec((m, n), lambda b, s: (0, 0))
    tile_f32 = pltpu.VMEM((ts, RWKV_WIDTH), F32)
    n_chunks = ts // RWKV_CHUNK
    n_groups = RWKV_WIDTH // GROUP_LANES
    return pl.pallas_call(
        _rwkv_kernel,
        grid=(bsz, seq // ts),
        in_specs=[
            pl.BlockSpec((1, ts, RWKV_COLS), lambda b, s: (b, s, 0)),
            vec(RWKV_COLS), vec(RWKV_WIDTH), mat(DECAY_LORA + ICL_LORA, RWKV_WIDTH),
            vec(RWKV_WIDTH), mat(DECAY_LORA + ICL_LORA, RWKV_WIDTH), mat(GATE_LORA, RWKV_WIDTH),
            vec(RWKV_WIDTH), vec(RWKV_WIDTH), vec(RWKV_WIDTH), vec(RWKV_WIDTH), vec(RWKV_WIDTH),
        ],
        out_specs=pl.BlockSpec((1, ts, RWKV_WIDTH), lambda b, s: (b, s, 0)),
        out_shape=jax.ShapeDtypeStruct((bsz, seq, RWKV_WIDTH), F32),
        scratch_shapes=[
            pltpu.VMEM((1, RWKV_COLS), F32),
            pltpu.VMEM((n_groups, GROUP_LANES, GROUP_LANES), F32),
            tile_f32, tile_f32, tile_f32, tile_f32, tile_f32, tile_f32, tile_f32, tile_f32,
            pltpu.VMEM((n_chunks, n_groups, 2 * GROUP_LANES, GROUP_LANES), BF16),
            pltpu.VMEM((n_chunks, n_groups, GROUP_LANES, GROUP_LANES), F32),
            pltpu.VMEM((n_chunks, n_groups, RWKV_CHUNK, GROUP_LANES), F32),
            pltpu.VMEM((n_chunks, n_groups, GROUP_LANES, GROUP_LANES), BF16),
            pltpu.VMEM((n_chunks, n_groups, 2 * RWKV_CHUNK, GROUP_LANES), BF16),
            pltpu.VMEM((n_chunks, 1, RWKV_WIDTH), F32),
        ],
        compiler_params=pltpu.CompilerParams(
            dimension_semantics=("arbitrary", "arbitrary"), vmem_limit_bytes=VMEM_LIMIT_BYTES),
        name="rwkv7",
    )(proj3d, mu, w0, wb_pad, a0, ab_pad, gb, kkw, ka, rk, lng, lnb)


def _rwkv_stage(proj3d, p):
    row = lambda t: t.reshape(1, -1)
    zeros = jnp.zeros((DECAY_LORA, RWKV_WIDTH), F32)
    wb_pad = jnp.concatenate([p['rwkv_wb'][0], zeros], axis=0).astype(BF16)
    ab_pad = jnp.concatenate([zeros, p['rwkv_ab'][0]], axis=0).astype(BF16)
    return _rwkv(proj3d, row(p['rwkv_mu'][0]), row(p['rwkv_w0'][0]), wb_pad, row(p['rwkv_a0'][0]), ab_pad,
                 p['rwkv_gb'][0].astype(BF16), row(p['rwkv_kk'][0]), row(p['rwkv_ka'][0]),
                 row(p['rwkv_rk'][0]), row(p['rwkv_ln_g'][0]), row(p['rwkv_ln_b'][0]))


ATT_TILE = 256
ATT_LOCKSTEP = 2
ATT_AUG = 2 * LANES
POS_SPLIT_BITS = 6
MASK_VALUE = -1e30


def _alibi_lanes(n, pos0, slope, key_side):
    pos = pos0 + lax.broadcasted_iota(jnp.int32, (n, LANES), 0)
    lane = lax.broadcasted_iota(jnp.int32, (n, LANES), 1)
    hi = (pos >> POS_SPLIT_BITS).astype(F32) * (slope * float(1 << POS_SPLIT_BITS))
    lo = (pos & ((1 << POS_SPLIT_BITS) - 1)).astype(F32) * slope
    if key_side:
        return jnp.where(lane == 0, hi, jnp.where(lane == 1, lo, jnp.where(lane < 4, 1.0, 0.0)))
    return jnp.where(lane < 2, 1.0, jnp.where(lane == 2, -hi, jnp.where(lane == 3, -lo, 0.0)))


def _diff_kernel(slope_ref, lq1_ref, lk1_ref, lq2_ref, lk2_ref, g_ref, *refs):
    nh = DIFF_HEADS
    q_refs, k_refs, v_refs = refs[:nh], refs[nh:2 * nh], refs[2 * nh:3 * nh]
    o_ref, ks_ref, vs_ref, m_ref, acc_ref = refs[3 * nh:]
    tq = q_refs[0].shape[1]
    seq = k_refs[0].shape[1]
    qi = pl.program_id(1)
    slopes = [slope_ref[h][:, :1] for h in range(nh)]

    @pl.when(qi == 0)
    def _():
        for h in range(nh):
            ks_ref[h, :, :LANES] = k_refs[h][0].astype(BF16)
            ks_ref[h, :, LANES:] = _alibi_lanes(seq, 0, slopes[h], True).astype(BF16)
            vs_ref[h, :, :LANES] = v_refs[h][0].astype(BF16)
            vs_ref[h, :, LANES:] = jnp.ones((seq, LANES), BF16)

    lane = lax.broadcasted_iota(jnp.int32, (1, LANES), 1)
    qa = []
    for h in range(nh):
        q = q_refs[h][0] * (DIFF_HEAD ** -0.5)
        q_terms = _alibi_lanes(tq, qi * tq, slopes[h], False)
        qa.append(jnp.concatenate(
            [jnp.concatenate([jnp.where(lane < DIFF_HEAD, q, 0.0), q_terms], axis=1),
             jnp.concatenate([jnp.where(lane >= DIFF_HEAD, q, 0.0), q_terms], axis=1)],
            axis=0).astype(BF16))
    m_ref[...] = jnp.full_like(m_ref, MASK_VALUE)
    acc_ref[...] = jnp.zeros_like(acc_ref)
    causal = (lax.broadcasted_iota(jnp.int32, (tq, tq), 0)
              >= lax.broadcasted_iota(jnp.int32, (tq, tq), 1))
    causal = jnp.concatenate([causal, causal], axis=0)

    def process(j, diagonal):
        rows = pl.ds(pl.multiple_of(j * tq, tq), tq)
        for h0 in range(0, nh, ATT_LOCKSTEP):
            hs = range(h0, h0 + ATT_LOCKSTEP)
            s = {h: _dot_nt(qa[h], ks_ref[h, rows, :]) for h in hs}
            if diagonal:
                s = {h: jnp.where(causal, s[h], MASK_VALUE) for h in hs}
            m_old = {h: m_ref[h] for h in hs}
            m_new = {h: jnp.maximum(m_old[h], jnp.max(s[h], axis=-1, keepdims=True)) for h in hs}
            pr = {h: jnp.exp(s[h] - jnp.concatenate([m_new[h], m_new[h]], axis=1)).astype(BF16)
                  for h in hs}
            pv = {h: _dot(pr[h], vs_ref[h, rows, :]) for h in hs}
            for h in hs:
                alpha = jnp.exp(m_old[h] - m_new[h])
                acc_ref[h] = jnp.concatenate([alpha, alpha], axis=1) * acc_ref[h] + pv[h]
                m_ref[h] = m_new[h]

    def off_diagonal(j, carry):
        process(j, False)
        return carry

    lax.fori_loop(0, qi, off_diagonal, 0)
    process(qi, True)

    lam = (jnp.exp(jnp.sum(lq1_ref[...] * lk1_ref[...], axis=-1, keepdims=True))
           - jnp.exp(jnp.sum(lq2_ref[...] * lk2_ref[...], axis=-1, keepdims=True)) + LAMBDA_INIT)
    for h in range(nh):
        acc = acc_ref[h]
        o12 = acc[:, :DIFF_VDIM] / acc[:, DIFF_VDIM:]
        o = o12[:tq] - lam * o12[tq:]
        o_ref[0, :, h * DIFF_VDIM:(h + 1) * DIFF_VDIM] = _rms(o, g_ref[...]) * (1.0 - LAMBDA_INIT)


def _diff(proj3d, slopes, lq1, lk1, lq2, lk2, subln_g):
    bsz, seq, _ = proj3d.shape
    tq = ATT_TILE
    q_blk0 = RWKV_COLS // DIFF_VDIM
    k_blk0 = q_blk0 + DIFF_HEADS
    v_blk0 = k_blk0 + DIFF_HEADS
    nh = DIFF_HEADS
    vec = lambda n: pl.BlockSpec((1, n), lambda b, i: (0, 0))
    q_spec = lambda h: pl.BlockSpec((1, tq, DIFF_VDIM), lambda b, i: (b, i, q_blk0 + h))
    kv_spec = lambda blk: pl.BlockSpec((1, seq, DIFF_VDIM), lambda b, i: (b, 0, blk))
    return pl.pallas_call(
        _diff_kernel,
        grid=(bsz, seq // tq),
        in_specs=([pl.BlockSpec((nh, 1, LANES), lambda b, i: (0, 0, 0)),
                   vec(DIFF_HEAD), vec(DIFF_HEAD), vec(DIFF_HEAD), vec(DIFF_HEAD), vec(DIFF_VDIM)]
                  + [q_spec(h) for h in range(nh)]
                  + [kv_spec(k_blk0 + h) for h in range(nh)]
                  + [kv_spec(v_blk0 + h) for h in range(nh)]),
        out_specs=pl.BlockSpec((1, tq, DIFF_WIDTH), lambda b, i: (b, i, 0)),
        out_shape=jax.ShapeDtypeStruct((bsz, seq, DIFF_WIDTH), F32),
        scratch_shapes=[pltpu.VMEM((nh, seq, ATT_AUG), BF16), pltpu.VMEM((nh, seq, ATT_AUG), BF16),
                        pltpu.VMEM((nh, 2 * tq, LANES), F32), pltpu.VMEM((nh, 2 * tq, ATT_AUG), F32)],
        compiler_params=pltpu.CompilerParams(
            dimension_semantics=("arbitrary", "arbitrary"), vmem_limit_bytes=VMEM_LIMIT_BYTES),
        name="diffattn",
    )(slopes, lq1, lk1, lq2, lk2, subln_g, *([proj3d] * (3 * nh)))


def _diff_stage(proj3d, p):
    row = lambda t: t.reshape(1, -1)
    slopes = 2.0 ** (-8.0 * (jnp.arange(DIFF_HEADS, dtype=F32) + 1.0) / DIFF_HEADS)
    slopes = jnp.broadcast_to(slopes[:, None, None], (DIFF_HEADS, 1, LANES))
    return _diff(proj3d, slopes, row(p['diff_lq1'][0]), row(p['diff_lk1'][0]), row(p['diff_lq2'][0]),
                 row(p['diff_lk2'][0]), row(p['diff_subln_g'][0]))


PACK_COLS = D_MODEL // 2
PACK_CHUNKS = PACK_COLS // LANES
OUT_CHUNKS = D_MODEL // LANES


def _pack_bf16_pairs(x):
    xb = x.astype(BF16).astype(F32)
    lo = lax.bitcast_convert_type(xb[:, :PACK_COLS], jnp.uint32) >> 16
    hi = lax.bitcast_convert_type(xb[:, PACK_COLS:], jnp.uint32)
    return lo | hi


def _unpack_bf16_pairs(xp):
    lo = lax.bitcast_convert_type(xp << 16, F32)
    hi = lax.bitcast_convert_type(xp & jnp.uint32(0xFFFF0000), F32)
    return jnp.concatenate([lo, hi], axis=1).astype(BF16)


def _store_row_chunks(ref, value):
    n_rows, width = value.shape
    n_chunks = width // LANES
    for c in range(n_chunks):
        ref[pl.ds(c, n_rows, stride=n_chunks), :] = value[:, c * LANES:(c + 1) * LANES]


def _load_row_chunks(ref, n_rows, n_chunks, first_row=0):
    return jnp.concatenate(
        [ref[pl.ds(first_row * n_chunks + c, n_rows, stride=n_chunks), :] for c in range(n_chunks)], axis=1)


def _row_chunks(ref, row, n_chunks):
    return ref.at[pl.ds(pl.multiple_of(row * n_chunks, n_chunks), n_chunks), :]


def _mix_router_kernel(x_ref, yr_ref, yd_ref, wt_ref, wb_ref, g_ref, rwh_ref, rwl_ref, rb_ref,
                       h_ref, xp_ref, ti_ref, gt_ref, rk_ref, cnt_ref):
    tm = x_ref.shape[0]

    @pl.when(pl.program_id(0) == 0)
    def _():
        cnt_ref[...] = jnp.zeros_like(cnt_ref)

    h = (x_ref[...] + _dot(yr_ref[...].astype(BF16), wt_ref[...])
         + _dot(yd_ref[...].astype(BF16), wb_ref[...]))
    h_ref[...] = h
    xn = _rms(h, g_ref[...])
    _store_row_chunks(xp_ref, _pack_bf16_pairs(xn))
    x_hi, x_lo = _split2(xn)
    logits = (_dot(x_hi, rwh_ref[...]) + _dot(x_hi, rwl_ref[...]) + _dot(x_lo, rwh_ref[...])
              + rb_ref[...])

    lane_e = lax.broadcasted_iota(jnp.int32, (tm, N_EXPERTS), 1)
    lane_k = lax.broadcasted_iota(jnp.int32, (tm, TOP_K), 1)
    vals = logits
    tops, sels = [], []
    top_i = jnp.zeros((tm, TOP_K), jnp.int32)
    for k in range(TOP_K):
        m = jnp.max(vals, axis=-1, keepdims=True)
        idx = jnp.min(jnp.where(vals == m, lane_e, N_EXPERTS), axis=-1, keepdims=True)
        sel = lane_e == idx
        vals = jnp.where(sel, -jnp.inf, vals)
        tops.append(m)
        sels.append(sel)
        top_i = jnp.where(lane_k == k, idx, top_i)
    ti_ref[...] = top_i

    exps = [jnp.exp(m - tops[0]) for m in tops]
    den = exps[0] + exps[1] + exps[2] + exps[3]
    gates = jnp.zeros((tm, TOP_K), F32)
    for k in range(TOP_K):
        gates = jnp.where(lane_k == k, exps[k] / den, gates)
    gt_ref[...] = gates

    cnt = jnp.zeros((tm, N_EXPERTS), F32)
    for sel in sels:
        cnt = cnt + sel.astype(F32)
    ri = lax.broadcasted_iota(jnp.int32, (tm, tm), 0)
    rj = lax.broadcasted_iota(jnp.int32, (tm, tm), 1)
    before = (rj < ri).astype(BF16)
    prefix = _dot(before, cnt.astype(BF16)) + cnt_ref[...]
    rank = jnp.zeros((tm, TOP_K), F32)
    for k in range(TOP_K):
        rk = jnp.sum(jnp.where(sels[k], prefix, 0.0), axis=-1, keepdims=True)
        rank = jnp.where(lane_k == k, rk, rank)
    rk_ref[...] = rank.astype(jnp.int32)
    cnt_ref[...] += jnp.sum(cnt, axis=0, keepdims=True)


def _mix_router(x2d, yr2d, yd2d, w_top, w_bot, g2, rw_hi, rw_lo, rb):
    n_tok = x2d.shape[0]
    tm = ROW_TILE
    rows = lambda n: pl.BlockSpec((tm, n), lambda i: (i, 0))
    full = lambda m, n: pl.BlockSpec((m, n), lambda i: (0, 0))
    return pl.pallas_call(
        _mix_router_kernel,
        grid=(n_tok // tm,),
        in_specs=[rows(D_MODEL), rows(RWKV_WIDTH), rows(DIFF_WIDTH),
                  full(RWKV_WIDTH, D_MODEL), full(DIFF_WIDTH, D_MODEL), full(1, D_MODEL),
                  full(D_MODEL, N_EXPERTS), full(D_MODEL, N_EXPERTS), full(1, N_EXPERTS)],
        out_specs=[rows(D_MODEL), pl.BlockSpec((tm * PACK_CHUNKS, LANES), lambda i: (i, 0)),
                   rows(TOP_K), rows(TOP_K), rows(TOP_K),
                   full(1, N_EXPERTS)],
        out_shape=[jax.ShapeDtypeStruct((n_tok, D_MODEL), F32),
                   jax.ShapeDtypeStruct((n_tok * PACK_CHUNKS, LANES), jnp.uint32),
                   jax.ShapeDtypeStruct((n_tok, TOP_K), jnp.int32),
                   jax.ShapeDtypeStruct((n_tok, TOP_K), F32),
                   jax.ShapeDtypeStruct((n_tok, TOP_K), jnp.int32),
                   jax.ShapeDtypeStruct((1, N_EXPERTS), F32)],
        compiler_params=pltpu.CompilerParams(
            dimension_semantics=("arbitrary",), vmem_limit_bytes=VMEM_LIMIT_BYTES),
        name="mix_router",
    )(x2d, yr2d, yd2d, w_top, w_bot, g2, rw_hi, rw_lo, rb)


EXPERT_TILE = 512
DISPATCH_TILE = 2048


def _dispatch_kernel(dest_ref, zero_ref, xp_ref, xs_ref, zbuf, sem, zsem):
    tm = xp_ref.shape[0] // PACK_CHUNKS
    blk_rows = EXPERT_TILE * PACK_CHUNKS

    @pl.when(pl.program_id(0) == 0)
    def _():
        zbuf[...] = jnp.zeros_like(zbuf)
        zero_block = lambda i: pltpu.make_async_copy(
            zbuf, xs_ref.at[pl.ds(pl.multiple_of(zero_ref[0, i] * blk_rows, blk_rows), blk_rows), :], zsem)
        for i in range(zero_ref.shape[1]):
            pl.when(zero_ref[0, i] >= 0)(lambda i=i: zero_block(i).start())
        for i in range(zero_ref.shape[1]):
            pl.when(zero_ref[0, i] >= 0)(lambda i=i: zero_block(i).wait())

    def issue(t, carry):
        src = _row_chunks(xp_ref, t, PACK_CHUNKS)
        for k in range(TOP_K):
            d = dest_ref[0, 0, t * TOP_K + k]
            pltpu.make_async_copy(src, _row_chunks(xs_ref, d, PACK_CHUNKS), sem).start(priority=k % 2)
        return carry

    lax.fori_loop(0, tm, issue, 0, unroll=8)
    for k in range(TOP_K):
        pltpu.make_async_copy(xp_ref, xs_ref.at[pl.ds(0, tm * PACK_CHUNKS), :], sem).wait()


def _dispatch(dest3d, zero_blocks, xp, n_slots):
    n_tok = xp.shape[0] // PACK_CHUNKS
    tm = DISPATCH_TILE
    return pl.pallas_call(
        _dispatch_kernel,
        grid=(n_tok // tm,),
        in_specs=[pl.BlockSpec((1, 1, tm * TOP_K), lambda i: (i, 0, 0), memory_space=pltpu.SMEM),
                  pl.BlockSpec(zero_blocks.shape, lambda i: (0, 0), memory_space=pltpu.SMEM),
                  pl.BlockSpec((tm * PACK_CHUNKS, LANES), lambda i: (i, 0))],
        out_specs=pl.BlockSpec(memory_space=pl.ANY),
        out_shape=jax.ShapeDtypeStruct((n_slots * PACK_CHUNKS, LANES), jnp.uint32),
        scratch_shapes=[pltpu.VMEM((EXPERT_TILE * PACK_CHUNKS, LANES), jnp.uint32),
                        pltpu.SemaphoreType.DMA(()), pltpu.SemaphoreType.DMA(())],
        compiler_params=pltpu.CompilerParams(
            dimension_semantics=("arbitrary",), vmem_limit_bytes=VMEM_LIMIT_BYTES),
        name="dispatch",
    )(dest3d, zero_blocks, xp)


SPLIT_BLOCK = 2 * LANES


def _expert_kernel(be_ref, src_ref, nu_ref, new_ref, xs_ref, wup_ref, wdn_ref, bg_ref, bl_ref, bd_ref,
                   y_ref, wg_s, wl_s, wd_s):
    del be_ref, src_ref
    j = pl.program_id(0)
    used = j < nu_ref[0]

    @pl.when(jnp.logical_not(used))
    def _():
        y_ref[...] = jnp.zeros_like(y_ref)

    @pl.when(jnp.logical_and(used, new_ref[j] == 1))
    def _():
        src = lax.broadcasted_iota(jnp.int32, (SPLIT_BLOCK, SPLIT_BLOCK), 0)
        dst = lax.broadcasted_iota(jnp.int32, (SPLIT_BLOCK, SPLIT_BLOCK), 1)
        perm = (src == jnp.where(dst < LANES, 2 * dst, 2 * (dst - LANES) + 1)).astype(BF16)
        for c in range(2 * D_FF // SPLIT_BLOCK):
            blk = wup_ref[0, :, c * SPLIT_BLOCK:(c + 1) * SPLIT_BLOCK].astype(BF16)
            out = _dot(blk, perm).astype(BF16)
            wg_s[:, c * LANES:(c + 1) * LANES] = out[:, :LANES]
            wl_s[:, c * LANES:(c + 1) * LANES] = out[:, LANES:]
        wd_s[...] = wdn_ref[0].astype(BF16)

    @pl.when(used)
    def _():
        x = _unpack_bf16_pairs(_load_row_chunks(xs_ref, EXPERT_TILE, PACK_CHUNKS))
        glu = jnp.minimum(_dot(x, wg_s[...]) + bg_ref[0], SWIGLU_LIMIT)
        lin = jnp.clip(_dot(x, wl_s[...]) + bl_ref[0], -SWIGLU_LIMIT, SWIGLU_LIMIT)
        act = glu * jax.nn.sigmoid(SWIGLU_ALPHA * glu) * (lin + 1.0)
        _store_row_chunks(y_ref, _dot(act.astype(BF16), wd_s[...]) + bd_ref[0])


def _experts(blk_expert, blk_src, n_used, blk_new, xs, w_up, w_down, bg, bl, bd):
    n_slots = xs.shape[0] // PACK_CHUNKS
    tb = EXPERT_TILE
    wspec = lambda k, n: pl.BlockSpec((1, k, n), lambda j, be, src, nu, new: (be[j], 0, 0))
    return pl.pallas_call(
        _expert_kernel,
        grid_spec=pltpu.PrefetchScalarGridSpec(
            num_scalar_prefetch=4,
            grid=(n_slots // tb,),
            in_specs=[pl.BlockSpec((tb * PACK_CHUNKS, LANES), lambda j, be, src, nu, new: (src[j], 0)),
                      wspec(D_MODEL, 2 * D_FF), wspec(D_FF, D_MODEL),
                      wspec(1, D_FF), wspec(1, D_FF), wspec(1, D_MODEL)],
            out_specs=pl.BlockSpec((tb * OUT_CHUNKS, LANES), lambda j, be, src, nu, new: (j, 0)),
            scratch_shapes=[pltpu.VMEM((D_MODEL, D_FF), BF16), pltpu.VMEM((D_MODEL, D_FF), BF16),
                            pltpu.VMEM((D_FF, D_MODEL), BF16)],
        ),
        out_shape=jax.ShapeDtypeStruct((n_slots * OUT_CHUNKS, LANES), F32),
        compiler_params=pltpu.CompilerParams(
            dimension_semantics=("arbitrary",), vmem_limit_bytes=VMEM_LIMIT_BYTES),
        name="experts",
    )(blk_expert, blk_src, n_used, blk_new, xs, w_up, w_down, bg, bl, bd)


COMBINE_TILE = 1024


def _combine_kernel(dest_ref, h_ref, gt_ref, g_ref, ys_ref, o_ref, buf, sem):
    tm = h_ref.shape[0]

    def issue(t, carry):
        for k in range(TOP_K):
            d = dest_ref[0, 0, t * TOP_K + k]
            pltpu.make_async_copy(_row_chunks(ys_ref, d, OUT_CHUNKS),
                                  _row_chunks(buf, k * tm + t, OUT_CHUNKS), sem).start(priority=k % 2)
        return carry

    lax.fori_loop(0, tm, issue, 0, unroll=8)
    rows_per_k = tm * OUT_CHUNKS
    for k in range(TOP_K):
        pltpu.make_async_copy(ys_ref.at[pl.ds(0, rows_per_k), :],
                              buf.at[pl.ds(k * rows_per_k, rows_per_k), :], sem).wait()
    gates = gt_ref[...]
    out = h_ref[...]
    for k in range(TOP_K):
        out = out + gates[:, k:k + 1] * _load_row_chunks(buf, tm, OUT_CHUNKS, first_row=k * tm)
    o_ref[...] = _rms(out, g_ref[...])


def _combine(dest3d, h, gates, final_g, ys):
    n_tok = h.shape[0]
    tm = COMBINE_TILE
    return pl.pallas_call(
        _combine_kernel,
        grid=(n_tok // tm,),
        in_specs=[pl.BlockSpec((1, 1, tm * TOP_K), lambda i: (i, 0, 0), memory_space=pltpu.SMEM),
                  pl.BlockSpec((tm, D_MODEL), lambda i: (i, 0)),
                  pl.BlockSpec((tm, TOP_K), lambda i: (i, 0)),
                  pl.BlockSpec((1, D_MODEL), lambda i: (0, 0)),
                  pl.BlockSpec(memory_space=pl.ANY)],
        out_specs=pl.BlockSpec((tm, D_MODEL), lambda i: (i, 0)),
        out_shape=jax.ShapeDtypeStruct((n_tok, D_MODEL), F32),
        scratch_shapes=[pltpu.VMEM((TOP_K * tm * OUT_CHUNKS, LANES), F32), pltpu.SemaphoreType.DMA(())],
        compiler_params=pltpu.CompilerParams(
            dimension_semantics=("arbitrary",), vmem_limit_bytes=VMEM_LIMIT_BYTES),
        name="combine",
    )(dest3d, h, gates, final_g, ys)


def _moe_stage(x2d, yr2d, yd2d, p):
    n_tok = x2d.shape[0]
    row = lambda t: t.reshape(1, -1)
    w_out = p['w_out'][0].astype(BF16)
    rw = p['router_w'][0]
    rw_hi = rw.astype(BF16)
    rw_lo = (rw - rw_hi.astype(F32)).astype(BF16)
    h, xp, top_i, gates, rank, counts = _mix_router(
        x2d, yr2d, yd2d, w_out[:RWKV_WIDTH], w_out[RWKV_WIDTH:], row(p['norm2_g'][0]),
        rw_hi, rw_lo, row(p['router_b'][0]))

    tb = EXPERT_TILE
    n_blocks = (n_tok * TOP_K) // tb + N_EXPERTS
    counts = counts[0].astype(jnp.int32)
    padded = (counts + tb - 1) // tb * tb
    experts = jnp.arange(N_EXPERTS, dtype=jnp.int32)
    pad_end = jnp.sum(jnp.where(experts[None, :] <= experts[:, None], padded[None, :], 0), axis=1)
    pad_start = pad_end - padded
    dest = rank + jnp.sum(jnp.where(top_i[..., None] == experts, pad_start, 0), axis=-1)
    n_used = pad_end[-1] // tb
    blk_src = jnp.minimum(jnp.arange(n_blocks, dtype=jnp.int32), n_used - 1)
    blk_expert = jnp.minimum(
        jnp.sum((pad_end[None, :] <= (blk_src * tb)[:, None]).astype(jnp.int32), axis=1), N_EXPERTS - 1)

    last_blk = jnp.where(padded > 0, pad_end // tb - 1, -1)
    tail_blk = jnp.where(n_used + experts < n_blocks, n_used + experts, -1)
    zero_blocks = jnp.concatenate([last_blk, tail_blk]).astype(jnp.int32).reshape(1, 2 * N_EXPERTS)
    xs = _dispatch(dest.reshape(n_tok // DISPATCH_TILE, 1, DISPATCH_TILE * TOP_K), zero_blocks, xp,
                   n_blocks * tb)

    b_up = p['exp_b_up'][0]
    bg, bl = b_up[:, None, 0::2], b_up[:, None, 1::2]
    blk_new = jnp.concatenate([jnp.ones((1,), jnp.int32),
                               (blk_expert[1:] != blk_expert[:-1]).astype(jnp.int32)])
    ys = _experts(blk_expert, blk_src.astype(jnp.int32), n_used.reshape(1).astype(jnp.int32), blk_new, xs,
                  p['exp_w_up'][0], p['exp_w_down'][0], bg, bl, p['exp_b_down'][0][:, None, :])
    return _combine(dest.reshape(n_tok // COMBINE_TILE, 1, COMBINE_TILE * TOP_K), h, gates,
                    row(p['final_g']), ys)


def _moe_stage_test(h2d, p):
    zeros = jnp.zeros((h2d.shape[0], RWKV_WIDTH), F32)
    return _moe_stage(h2d, zeros, zeros, p)


def kernel(x, norm1_g, w_in, rwkv_mu, rwkv_w0, rwkv_wb, rwkv_a0, rwkv_ab, rwkv_gb, rwkv_kk, rwkv_ka, rwkv_rk, rwkv_ln_g, rwkv_ln_b, diff_lq1, diff_lk1, diff_lq2, diff_lk2, diff_subln_g, w_out, norm2_g, router_w, router_b, exp_w_up, exp_b_up, exp_w_down, exp_b_down, final_g):
    p = dict(rwkv_mu=rwkv_mu, rwkv_w0=rwkv_w0, rwkv_wb=rwkv_wb, rwkv_a0=rwkv_a0, rwkv_ab=rwkv_ab,
             rwkv_gb=rwkv_gb, rwkv_kk=rwkv_kk, rwkv_ka=rwkv_ka, rwkv_rk=rwkv_rk,
             rwkv_ln_g=rwkv_ln_g, rwkv_ln_b=rwkv_ln_b, diff_lq1=diff_lq1, diff_lk1=diff_lk1,
             diff_lq2=diff_lq2, diff_lk2=diff_lk2, diff_subln_g=diff_subln_g, w_out=w_out,
             norm2_g=norm2_g, router_w=router_w, router_b=router_b, exp_w_up=exp_w_up,
             exp_b_up=exp_b_up, exp_w_down=exp_w_down, exp_b_down=exp_b_down, final_g=final_g)
    bsz, seq, d = x.shape
    n_tok = bsz * seq
    x2d = x.reshape(n_tok, d)
    proj = _inproj(x2d, norm1_g, w_in[0].astype(BF16)).reshape(bsz, seq, IN_COLS)
    y_rwkv = _rwkv_stage(proj, p).reshape(n_tok, RWKV_WIDTH)
    y_diff = _diff_stage(proj, p).reshape(n_tok, DIFF_WIDTH)
    return _moe_stage(x2d, y_rwkv, y_diff, p).reshape(bsz, seq, d)
```

```python
import functools
import math

import jax
import jax.numpy as jnp
from jax import lax
from jax.experimental import pallas as pl
from jax.experimental.pallas import tpu as pltpu

F32 = jnp.float32
BF16 = jnp.bfloat16

D_MODEL = 1024
RWKV_WIDTH = 512
RWKV_HEAD = 64
DECAY_LORA = 64
ICL_LORA = 64
GATE_LORA = 128
GN_EPS = 64e-5
DIFF_WIDTH = 512
DIFF_HEAD = 64
DIFF_HEADS = 4
DIFF_VDIM = 128
RWKV_COLS = 3 * RWKV_WIDTH + DECAY_LORA + ICL_LORA + GATE_LORA
IN_COLS = RWKV_COLS + 3 * DIFF_WIDTH
N_EXPERTS = 32
TOP_K = 4
D_FF = 1024
SWIGLU_ALPHA = 1.702
SWIGLU_LIMIT = 7.0
NORM_EPS = 1e-5
LAMBDA_INIT = 0.8 - 0.6 * math.exp(-0.0)

LANES = 128
VMEM_LIMIT_BYTES = 56 * 1024 * 1024

ROW_TILE = 512
RWKV_CHUNK = 64
RWKV_GROUP = 4
GROUP_LANES = RWKV_GROUP * RWKV_HEAD
PREP_UNROLL = 2


def _dot(a, b):
    return jnp.dot(a, b, preferred_element_type=F32)


def _dot_nt(a, b):
    return lax.dot_general(a, b, (((1,), (1,)), ((), ())), preferred_element_type=F32)


def _dot_tn(a, b):
    return lax.dot_general(a, b, (((0,), (0,)), ((), ())), preferred_element_type=F32)


def _split2(x):
    hi = x.astype(BF16)
    lo = (x - hi.astype(F32)).astype(BF16)
    return hi, lo


def _rms(x, g):
    return x * lax.rsqrt(jnp.mean(x * x, axis=-1, keepdims=True) + NORM_EPS) * g


def _inproj_kernel(x_ref, g_ref, w_ref, o_ref):
    xn = _rms(x_ref[...], g_ref[...])
    o_ref[...] = _dot(xn.astype(BF16), w_ref[...])


def _inproj(x2d, g, w_bf16):
    n_tok = x2d.shape[0]
    return pl.pallas_call(
        _inproj_kernel,
        grid=(n_tok // ROW_TILE,),
        in_specs=[
            pl.BlockSpec((ROW_TILE, D_MODEL), lambda i: (i, 0)),
            pl.BlockSpec((1, D_MODEL), lambda i: (0, 0)),
            pl.BlockSpec((D_MODEL, IN_COLS), lambda i: (0, 0)),
        ],
        out_specs=pl.BlockSpec((ROW_TILE, IN_COLS), lambda i: (i, 0)),
        out_shape=jax.ShapeDtypeStruct((n_tok, IN_COLS), F32),
        compiler_params=pltpu.CompilerParams(
            dimension_semantics=("arbitrary",), vmem_limit_bytes=VMEM_LIMIT_BYTES),
        name="inproj",
    )(x2d, g, w_bf16)


def _head_sum(x, bd_ones):
    xb = x.astype(BF16)
    return jnp.concatenate(
        [_dot(xb[:, g * GROUP_LANES:(g + 1) * GROUP_LANES], bd_ones)
         for g in range(RWKV_WIDTH // GROUP_LANES)], axis=1)


def _rwkv_kernel(p_ref, mu_ref, w0_ref, wb_ref, a0_ref, ab_ref, gb_ref, kkw_ref, ka_ref,
                 rk_ref, lng_ref, lnb_ref, o_ref,
                 carry_ref, state_ref, r_s, k_s, v_s, kk_s, b_s, lw_s, y_s, gate_s,
                 lhs_s, w0_s, yloc_s, arb_s, kb_s, dec_s):
    ts = p_ref.shape[1]
    n_chunks = ts // RWKV_CHUNK
    n_groups = RWKV_WIDTH // GROUP_LANES

    @pl.when(pl.program_id(1) == 0)
    def _():
        carry_ref[...] = jnp.zeros_like(carry_ref)
        state_ref[...] = jnp.zeros_like(state_ref)

    gi = lax.broadcasted_iota(jnp.int32, (GROUP_LANES, GROUP_LANES), 0)
    gj = lax.broadcasted_iota(jnp.int32, (GROUP_LANES, GROUP_LANES), 1)
    same_head = (gi // RWKV_HEAD) == (gj // RWKV_HEAD)
    strict = same_head & (gi % RWKV_CHUNK > gj % RWKV_CHUNK)
    incl = same_head & (gi % RWKV_CHUNK >= gj % RWKV_CHUNK)
    eye = (gi == gj).astype(F32)
    bd_ones = same_head.astype(BF16)

    pair_rows = PREP_UNROLL * RWKV_CHUNK
    first_row = lax.broadcasted_iota(jnp.int32, (pair_rows, 1), 0) == 0
    w1, w2, w3 = RWKV_WIDTH, 2 * RWKV_WIDTH, 3 * RWKV_WIDTH

    def project(cp):
        r0 = cp * pair_rows
        rows = slice(r0, r0 + pair_rows)

        def shifted(c0, c1):
            x = p_ref[0, rows, c0:c1]
            before = carry_ref[:, c0:c1] if cp == 0 else p_ref[0, r0 - 1:r0, c0:c1]
            prev = jnp.where(first_row, before, pltpu.roll(x, 1, 0))
            return x + (prev - x) * mu_ref[:, c0:c1]

        r = shifted(0, w1)
        k = shifted(w1, w2)
        v = shifted(w2, w3)
        yield
        wa_lo = shifted(w3, w3 + DECAY_LORA + ICL_LORA)
        g_lo = shifted(w3 + DECAY_LORA + ICL_LORA, RWKV_COLS)
        z = w0_ref[...] + _dot(jnp.tanh(wa_lo).astype(BF16), wb_ref[...])
        nz = -z
        softplus = jnp.maximum(nz, 0.0) + jnp.log1p(jnp.exp(-jnp.abs(nz)))
        lw_s[rows] = -jnp.exp(-softplus - 0.5)
        yield
        a = jax.nn.sigmoid(a0_ref[...] + _dot(wa_lo.astype(BF16), ab_ref[...]))
        gate_s[rows] = _dot(jax.nn.sigmoid(g_lo).astype(BF16), gb_ref[...])
        yield
        kk = k * kkw_ref[...]
        kk = kk * lax.rsqrt(jnp.maximum(_head_sum(kk * kk, bd_ones), 1e-24))
        r_s[rows] = r
        k_s[rows] = k * (1.0 + (a - 1.0) * ka_ref[...])
        v_s[rows] = v
        kk_s[rows] = kk
        b_s[rows] = kk * a

    def finish(cp):
        rows = slice(cp * pair_rows, (cp + 1) * pair_rows)
        y = y_s[rows]
        inv_n = 1.0 / RWKV_HEAD
        d = y - _head_sum(y, bd_ones) * inv_n
        yield
        var = _head_sum(d * d, bd_ones) * inv_n
        yn = d * lax.rsqrt(var + GN_EPS) * lng_ref[...] + lnb_ref[...]
        yield
        bonus = _head_sum(r_s[rows] * k_s[rows] * rk_ref[...], bd_ones) * v_s[rows]
        o_ref[0, rows] = (yn + bonus) * gate_s[rows]

    ci = lax.broadcasted_iota(jnp.int32, (RWKV_CHUNK, RWKV_CHUNK), 0)
    cj = lax.broadcasted_iota(jnp.int32, (RWKV_CHUNK, RWKV_CHUNK), 1)
    tri = (ci >= cj).astype(BF16)
    tri2 = jnp.concatenate([tri, tri], axis=1)

    def tile_heads(x):
        return jnp.concatenate([x] * RWKV_GROUP, axis=0)

    def fold_heads(x):
        out = x[0:RWKV_CHUNK]
        for i in range(1, RWKV_GROUP):
            out = out + x[i * RWKV_CHUNK:(i + 1) * RWKV_CHUNK]
        return out

    def chunk_rows(c):
        return slice(c * RWKV_CHUNK, (c + 1) * RWKV_CHUNK)

    def prepare(cp):
        units = [(cp * PREP_UNROLL + i, g) for i in range(PREP_UNROLL) for g in range(n_groups)]
        lanes = [slice(g * GROUP_LANES, (g + 1) * GROUP_LANES) for _, g in units]
        rows = [chunk_rows(c) for c, _ in units]
        load = lambda ref: [ref[rw, ln] for rw, ln in zip(rows, lanes)]
        rc, kc, vc, kkc, bc, lwc = (load(ref) for ref in (r_s, k_s, v_s, kk_s, b_s, lw_s))
        cum = [_dot(tri2, jnp.concatenate(_split2(x), axis=0)) for x in lwc]
        yield
        cum_end = [x[RWKV_CHUNK - 1:RWKV_CHUNK, :] for x in cum]
        for (c, _), ln, ce in zip(units, lanes, cum_end):
            dec_s[c, :, ln] = jnp.exp(ce)
        e_neg = [jnp.exp(-x) for x in cum]
        e_end = [jnp.exp(ce - x) for ce, x in zip(cum_end, cum)]
        kkd = [a * jnp.exp(x - lw) for a, x, lw in zip(kkc, cum, lwc)]
        rd = [a * jnp.exp(x) for a, x in zip(rc, cum)]
        spread = lambda xs: [jnp.where(same_head, tile_heads(x), 0.0).astype(BF16) for x in xs]
        kkd_s, rd_s, v_st = spread(kkd), spread(rd), spread(vc)
        rhs = [jnp.concatenate([tile_heads(b * e), tile_heads(k * e)], axis=0).astype(BF16)
               for b, k, e in zip(bc, kc, e_neg)]
        yield
        gram = [_dot_nt(jnp.concatenate([a, b], axis=0), w) for a, b, w in zip(kkd_s, rd_s, rhs)]
        yield
        gl = GROUP_LANES
        a_ab = [jnp.where(strict, x[:gl, :gl], 0.0) for x in gram]
        a_rb = [jnp.where(incl, x[gl:, :gl], 0.0).astype(BF16) for x in gram]
        a_ak = [jnp.where(strict, x[:gl, gl:], 0.0).astype(BF16) for x in gram]
        a_rk = [jnp.where(incl, x[gl:, gl:], 0.0).astype(BF16) for x in gram]
        av = [_dot(jnp.concatenate([a, b], axis=0), v) for a, b, v in zip(a_ak, a_rk, v_st)]
        yield

        pw = [x.astype(BF16) for x in a_ab]
        t_inv = [eye - x for x in a_ab]
        for _ in range(int(math.log2(RWKV_CHUNK)) - 1):
            pw = [_dot(x, x).astype(BF16) for x in pw]
            t_inv = [t + _dot(t.astype(BF16), x) for t, x in zip(t_inv, pw)]
            yield
        sol = [_dot(t.astype(BF16), jnp.concatenate([a, x[:gl].astype(BF16)], axis=1))
               for t, a, x in zip(t_inv, kkd_s, av)]
        yield
        for i, (c, g) in enumerate(units):
            lhs_s[c, g] = jnp.concatenate([sol[i][:, :gl].astype(BF16), rd_s[i]], axis=0)
            w0_s[c, g] = sol[i][:, gl:]
            yloc_s[c, g] = fold_heads(av[i][gl:])
            arb_s[c, g] = a_rb[i]
            kb_s[c, g] = jnp.concatenate([kc[i] * e_end[i], bc[i] * e_end[i]], axis=0).astype(BF16)

    def advance(c):
        rows = chunk_rows(c)
        gs = range(n_groups)
        lanes = [slice(g * GROUP_LANES, (g + 1) * GROUP_LANES) for g in gs]
        state = [state_ref[g] for g in gs]
        m1 = [_dot_nt(lhs_s[c, g], state[g].astype(BF16)) for g in gs]
        yield
        sa_st = [m1[g][:GROUP_LANES] + w0_s[c, g] for g in gs]
        y_st = [m1[g][GROUP_LANES:] - _dot(arb_s[c, g], sa_st[g].astype(BF16)) for g in gs]
        yield
        upd = [_dot_tn(jnp.concatenate([v_s[rows, lanes[g]], -fold_heads(sa_st[g])], axis=0).astype(BF16),
                       kb_s[c, g]) for g in gs]
        yield
        for g in gs:
            y_s[rows, lanes[g]] = fold_heads(y_st[g]) + yloc_s[c, g]
            state_ref[g] = state[g] * dec_s[c, :, lanes[g]] + jnp.where(same_head, upd[g], 0.0)
        yield

    def advance_chunks(cp):
        for i in range(PREP_UNROLL):
            yield from advance(cp * PREP_UNROLL + i)

    def interleave(*stages):
        live = list(stages)
        while live:
            for gen in list(live):
                if next(gen, StopIteration) is StopIteration:
                    live.remove(gen)

    n_pairs = n_chunks // PREP_UNROLL
    interleave(project(0))
    interleave(prepare(0), *([project(1)] if n_pairs > 1 else []))
    for cp in range(n_pairs):
        stages = [advance_chunks(cp)]
        if cp + 1 < n_pairs:
            stages.append(prepare(cp + 1))
        if cp + 2 < n_pairs:
            stages.append(project(cp + 2))
        if cp >= 1:
            stages.append(finish(cp - 1))
        interleave(*stages)
    interleave(finish(n_pairs - 1))
    carry_ref[...] = p_ref[0, ts - 1:ts, :]


def _rwkv(proj3d, mu, w0, wb_pad, a0, ab_pad, gb, kkw, ka, rk, lng, lnb):
    bsz, seq, _ = proj3d.shape
    ts = ROW_TILE
    vec = lambda n: pl.BlockSpec((1, n), lambda b, s: (0, 0))
    mat = lambda m, n: pl.BlockSpec((m, n), lambda b, s: (0, 0))
    tile_f32 = pltpu.VMEM((ts, RWKV_WIDTH), F32)
    n_chunks = ts // RWKV_CHUNK
    n_groups = RWKV_WIDTH // GROUP_LANES
    return pl.pallas_call(
        _rwkv_kernel,
        grid=(bsz, seq // ts),
        in_specs=[
            pl.BlockSpec((1, ts, RWKV_COLS), lambda b, s: (b, s, 0)),
            vec(RWKV_COLS), vec(RWKV_WIDTH), mat(DECAY_LORA + ICL_LORA, RWKV_WIDTH),
            vec(RWKV_WIDTH), mat(DECAY_LORA + ICL_LORA, RWKV_WIDTH), mat(GATE_LORA, RWKV_WIDTH),
            vec(RWKV_WIDTH), vec(RWKV_WIDTH), vec(RWKV_WIDTH), vec(RWKV_WIDTH), vec(RWKV_WIDTH),
        ],
        out_specs=pl.BlockSpec((1, ts, RWKV_WIDTH), lambda b, s: (b, s, 0)),
        out_shape=jax.ShapeDtypeStruct((bsz, seq, RWKV_WIDTH), F32),
        scratch_shapes=[
            pltpu.VMEM((1, RWKV_COLS), F32),
            pltpu.VMEM((n_groups, GROUP_LANES, GROUP_LANES), F32),
            tile_f32, tile_f32, tile_f32, tile_f32, tile_f32, tile_f32, tile_f32, tile_f32,
            pltpu.VMEM((n_chunks, n_groups, 2 * GROUP_LANES, GROUP_LANES), BF16),
            pltpu.VMEM((n_chunks, n_groups, GROUP_LANES, GROUP_LANES), F32),
            pltpu.VMEM((n_chunks, n_groups, RWKV_CHUNK, GROUP_LANES), F32),
            pltpu.VMEM((n_chunks, n_groups, GROUP_LANES, GROUP_LANES), BF16),
            pltpu.VMEM((n_chunks, n_groups, 2 * RWKV_CHUNK, GROUP_LANES), BF16),
            pltpu.VMEM((n_chunks, 1, RWKV_WIDTH), F32),
        ],
        compiler_params=pltpu.CompilerParams(
            dimension_semantics=("arbitrary", "arbitrary"), vmem_limit_bytes=VMEM_LIMIT_BYTES),
        name="rwkv7",
    )(proj3d, mu, w0, wb_pad, a0, ab_pad, gb, kkw, ka, rk, lng, lnb)


def _rwkv_stage(proj3d, p):
    row = lambda t: t.reshape(1, -1)
    zeros = jnp.zeros((DECAY_LORA, RWKV_WIDTH), F32)
    wb_pad = jnp.concatenate([p['rwkv_wb'][0], zeros], axis=0).astype(BF16)
    ab_pad = jnp.concatenate([zeros, p['rwkv_ab'][0]], axis=0).astype(BF16)
    return _rwkv(proj3d, row(p['rwkv_mu'][0]), row(p['rwkv_w0'][0]), wb_pad, row(p['rwkv_a0'][0]), ab_pad,
                 p['rwkv_gb'][0].astype(BF16), row(p['rwkv_kk'][0]), row(p['rwkv_ka'][0]),
                 row(p['rwkv_rk'][0]), row(p['rwkv_ln_g'][0]), row(p['rwkv_ln_b'][0]))


ATT_TILE = 512
ATT_LOCKSTEP = 2
ATT_AUG = 2 * LANES
POS_SPLIT_BITS = 6
MASK_VALUE = -1e30


def _alibi_lanes(n, pos0, slope, key_side):
    pos = pos0 + lax.broadcasted_iota(jnp.int32, (n, LANES), 0)
    lane = lax.broadcasted_iota(jnp.int32, (n, LANES), 1)
    hi = (pos >> POS_SPLIT_BITS).astype(F32) * (slope * float(1 << POS_SPLIT_BITS))
    lo = (pos & ((1 << POS_SPLIT_BITS) - 1)).astype(F32) * slope
    if key_side:
        return jnp.where(lane == 0, hi, jnp.where(lane == 1, lo, jnp.where(lane < 4, 1.0, 0.0)))
    return jnp.where(lane < 2, 1.0, jnp.where(lane == 2, -hi, jnp.where(lane == 3, -lo, 0.0)))


def _diff_kernel(slope_ref, lq1_ref, lk1_ref, lq2_ref, lk2_ref, g_ref, *refs):
    nh = DIFF_HEADS
    q_refs, k_refs, v_refs = refs[:nh], refs[nh:2 * nh], refs[2 * nh:3 * nh]
    o_ref, ks_ref, vs_ref, m_ref, acc_ref = refs[3 * nh:]
    tq = q_refs[0].shape[1]
    seq = k_refs[0].shape[1]
    qi = pl.program_id(1)
    slopes = [slope_ref[h][:, :1] for h in range(nh)]

    @pl.when(qi == 0)
    def _():
        for h in range(nh):
            ks_ref[h, :, :LANES] = k_refs[h][0].astype(BF16)
            ks_ref[h, :, LANES:] = _alibi_lanes(seq, 0, slopes[h], True).astype(BF16)
            vs_ref[h, :, :LANES] = v_refs[h][0].astype(BF16)
            vs_ref[h, :, LANES:] = jnp.ones((seq, LANES), BF16)

    lane = lax.broadcasted_iota(jnp.int32, (1, LANES), 1)
    qa = []
    for h in range(nh):
        q = q_refs[h][0] * (DIFF_HEAD ** -0.5)
        q_terms = _alibi_lanes(tq, qi * tq, slopes[h], False)
        qa.append(jnp.concatenate(
            [jnp.concatenate([jnp.where(lane < DIFF_HEAD, q, 0.0), q_terms], axis=1),
             jnp.concatenate([jnp.where(lane >= DIFF_HEAD, q, 0.0), q_terms], axis=1)],
            axis=0).astype(BF16))
    m_ref[...] = jnp.full_like(m_ref, MASK_VALUE)
    acc_ref[...] = jnp.zeros_like(acc_ref)
    causal = (lax.broadcasted_iota(jnp.int32, (tq, tq), 0)
              >= lax.broadcasted_iota(jnp.int32, (tq, tq), 1))
    causal = jnp.concatenate([causal, causal], axis=0)

    def process(j, diagonal):
        rows = pl.ds(pl.multiple_of(j * tq, tq), tq)
        for h0 in range(0, nh, ATT_LOCKSTEP):
            hs = range(h0, h0 + ATT_LOCKSTEP)
            s = {h: _dot_nt(qa[h], ks_ref[h, rows, :]) for h in hs}
            if diagonal:
                s = {h: jnp.where(causal, s[h], MASK_VALUE) for h in hs}
            m_old = {h: m_ref[h] for h in hs}
            m_new = {h: jnp.maximum(m_old[h], jnp.max(s[h], axis=-1, keepdims=True)) for h in hs}
            pr = {h: jnp.exp(s[h] - jnp.concatenate([m_new[h]] * (tq // LANES), axis=1)).astype(BF16)
                  for h in hs}
            pv = {h: _dot(pr[h], vs_ref[h, rows, :]) for h in hs}
            for h in hs:
                alpha = jnp.exp(m_old[h] - m_new[h])
                acc_ref[h] = jnp.concatenate([alpha, alpha], axis=1) * acc_ref[h] + pv[h]
                m_ref[h] = m_new[h]

    def off_diagonal(j, carry):
        process(j, False)
        return carry

    lax.fori_loop(0, qi, off_diagonal, 0)
    process(qi, True)

    lam = (jnp.exp(jnp.sum(lq1_ref[...] * lk1_ref[...], axis=-1, keepdims=True))
           - jnp.exp(jnp.sum(lq2_ref[...] * lk2_ref[...], axis=-1, keepdims=True)) + LAMBDA_INIT)
    for h in range(nh):
        acc = acc_ref[h]
        o12 = acc[:, :DIFF_VDIM] / acc[:, DIFF_VDIM:]
        o = o12[:tq] - lam * o12[tq:]
        o_ref[0, :, h * DIFF_VDIM:(h + 1) * DIFF_VDIM] = _rms(o, g_ref[...]) * (1.0 - LAMBDA_INIT)


def _diff(proj3d, slopes, lq1, lk1, lq2, lk2, subln_g):
    bsz, seq, _ = proj3d.shape
    tq = ATT_TILE
    q_blk0 = RWKV_COLS // DIFF_VDIM
    k_blk0 = q_blk0 + DIFF_HEADS
    v_blk0 = k_blk0 + DIFF_HEADS
    nh = DIFF_HEADS
    vec = lambda n: pl.BlockSpec((1, n), lambda b, i: (0, 0))
    q_spec = lambda h: pl.BlockSpec((1, tq, DIFF_VDIM), lambda b, i: (b, i, q_blk0 + h))
    kv_spec = lambda blk: pl.BlockSpec((1, seq, DIFF_VDIM), lambda b, i: (b, 0, blk))
    return pl.pallas_call(
        _diff_kernel,
        grid=(bsz, seq // tq),
        in_specs=([pl.BlockSpec((nh, 1, LANES), lambda b, i: (0, 0, 0)),
                   vec(DIFF_HEAD), vec(DIFF_HEAD), vec(DIFF_HEAD), vec(DIFF_HEAD), vec(DIFF_VDIM)]
                  + [q_spec(h) for h in range(nh)]
                  + [kv_spec(k_blk0 + h) for h in range(nh)]
                  + [kv_spec(v_blk0 + h) for h in range(nh)]),
        out_specs=pl.BlockSpec((1, tq, DIFF_WIDTH), lambda b, i: (b, i, 0)),
        out_shape=jax.ShapeDtypeStruct((bsz, seq, DIFF_WIDTH), F32),
        scratch_shapes=[pltpu.VMEM((nh, seq, ATT_AUG), BF16), pltpu.VMEM((nh, seq, ATT_AUG), BF16),
                        pltpu.VMEM((nh, 2 * tq, LANES), F32), pltpu.VMEM((nh, 2 * tq, ATT_AUG), F32)],
        compiler_params=pltpu.CompilerParams(
            dimension_semantics=("arbitrary", "arbitrary"), vmem_limit_bytes=VMEM_LIMIT_BYTES),
        name="diffattn",
    )(slopes, lq1, lk1, lq2, lk2, subln_g, *([proj3d] * (3 * nh)))


def _diff_stage(proj3d, p):
    row = lambda t: t.reshape(1, -1)
    slopes = 2.0 ** (-8.0 * (jnp.arange(DIFF_HEADS, dtype=F32) + 1.0) / DIFF_HEADS)
    slopes = jnp.broadcast_to(slopes[:, None, None], (DIFF_HEADS, 1, LANES))
    return _diff(proj3d, slopes, row(p['diff_lq1'][0]), row(p['diff_lk1'][0]), row(p['diff_lq2'][0]),
                 row(p['diff_lk2'][0]), row(p['diff_subln_g'][0]))


PACK_COLS = D_MODEL // 2
PACK_CHUNKS = PACK_COLS // LANES
OUT_CHUNKS = D_MODEL // LANES


def _pack_bf16_pairs(x):
    xb = x.astype(BF16).astype(F32)
    lo = lax.bitcast_convert_type(xb[:, :PACK_COLS], jnp.uint32) >> 16
    hi = lax.bitcast_convert_type(xb[:, PACK_COLS:], jnp.uint32)
    return lo | hi


def _unpack_bf16_pairs(xp):
    lo = lax.bitcast_convert_type(xp << 16, F32)
    hi = lax.bitcast_convert_type(xp & jnp.uint32(0xFFFF0000), F32)
    return jnp.concatenate([lo, hi], axis=1).astype(BF16)


def _store_row_chunks(ref, value):
    n_rows, width = value.shape
    n_chunks = width // LANES
    for c in range(n_chunks):
        ref[pl.ds(c, n_rows, stride=n_chunks), :] = value[:, c * LANES:(c + 1) * LANES]


def _load_row_chunks(ref, n_rows, n_chunks, first_row=0):
    return jnp.concatenate(
        [ref[pl.ds(first_row * n_chunks + c, n_rows, stride=n_chunks), :] for c in range(n_chunks)], axis=1)


def _row_chunks(ref, row, n_chunks):
    return ref.at[pl.ds(pl.multiple_of(row * n_chunks, n_chunks), n_chunks), :]


def _mix_router_kernel(x_ref, yr_ref, yd_ref, wt_ref, wb_ref, g_ref, rwh_ref, rwl_ref, rb_ref,
                       h_ref, xp_ref, ti_ref, gt_ref, rk_ref, cnt_ref):
    tm = x_ref.shape[0]

    @pl.when(pl.program_id(0) == 0)
    def _():
        cnt_ref[...] = jnp.zeros_like(cnt_ref)

    h = (x_ref[...] + _dot(yr_ref[...].astype(BF16), wt_ref[...])
         + _dot(yd_ref[...].astype(BF16), wb_ref[...]))
    h_ref[...] = h
    xn = _rms(h, g_ref[...])
    _store_row_chunks(xp_ref, _pack_bf16_pairs(xn))
    x_hi, x_lo = _split2(xn)
    logits = (_dot(x_hi, rwh_ref[...]) + _dot(x_hi, rwl_ref[...]) + _dot(x_lo, rwh_ref[...])
              + rb_ref[...])

    lane_e = lax.broadcasted_iota(jnp.int32, (tm, N_EXPERTS), 1)
    lane_k = lax.broadcasted_iota(jnp.int32, (tm, TOP_K), 1)
    vals = logits
    tops, sels = [], []
    top_i = jnp.zeros((tm, TOP_K), jnp.int32)
    for k in range(TOP_K):
        m = jnp.max(vals, axis=-1, keepdims=True)
        idx = jnp.min(jnp.where(vals == m, lane_e, N_EXPERTS), axis=-1, keepdims=True)
        sel = lane_e == idx
        vals = jnp.where(sel, -jnp.inf, vals)
        tops.append(m)
        sels.append(sel)
        top_i = jnp.where(lane_k == k, idx, top_i)
    ti_ref[...] = top_i

    exps = [jnp.exp(m - tops[0]) for m in tops]
    den = exps[0] + exps[1] + exps[2] + exps[3]
    gates = jnp.zeros((tm, TOP_K), F32)
    for k in range(TOP_K):
        gates = jnp.where(lane_k == k, exps[k] / den, gates)
    gt_ref[...] = gates

    cnt = jnp.zeros((tm, N_EXPERTS), F32)
    for sel in sels:
        cnt = cnt + sel.astype(F32)
    ri = lax.broadcasted_iota(jnp.int32, (tm, tm), 0)
    rj = lax.broadcasted_iota(jnp.int32, (tm, tm), 1)
    before = (rj < ri).astype(BF16)
    prefix = _dot(before, cnt.astype(BF16)) + cnt_ref[...]
    rank = jnp.zeros((tm, TOP_K), F32)
    for k in range(TOP_K):
        rk = jnp.sum(jnp.where(sels[k], prefix, 0.0), axis=-1, keepdims=True)
        rank = jnp.where(lane_k == k, rk, rank)
    rk_ref[...] = rank.astype(jnp.int32)
    cnt_ref[...] += jnp.sum(cnt, axis=0, keepdims=True)


def _mix_router(x2d, yr2d, yd2d, w_top, w_bot, g2, rw_hi, rw_lo, rb):
    n_tok = x2d.shape[0]
    tm = ROW_TILE
    rows = lambda n: pl.BlockSpec((tm, n), lambda i: (i, 0))
    full = lambda m, n: pl.BlockSpec((m, n), lambda i: (0, 0))
    return pl.pallas_call(
        _mix_router_kernel,
        grid=(n_tok // tm,),
        in_specs=[rows(D_MODEL), rows(RWKV_WIDTH), rows(DIFF_WIDTH),
                  full(RWKV_WIDTH, D_MODEL), full(DIFF_WIDTH, D_MODEL), full(1, D_MODEL),
                  full(D_MODEL, N_EXPERTS), full(D_MODEL, N_EXPERTS), full(1, N_EXPERTS)],
        out_specs=[rows(D_MODEL), pl.BlockSpec((tm * PACK_CHUNKS, LANES), lambda i: (i, 0)),
                   rows(TOP_K), rows(TOP_K), rows(TOP_K),
                   full(1, N_EXPERTS)],
        out_shape=[jax.ShapeDtypeStruct((n_tok, D_MODEL), F32),
                   jax.ShapeDtypeStruct((n_tok * PACK_CHUNKS, LANES), jnp.uint32),
                   jax.ShapeDtypeStruct((n_tok, TOP_K), jnp.int32),
                   jax.ShapeDtypeStruct((n_tok, TOP_K), F32),
                   jax.ShapeDtypeStruct((n_tok, TOP_K), jnp.int32),
                   jax.ShapeDtypeStruct((1, N_EXPERTS), F32)],
        compiler_params=pltpu.CompilerParams(
            dimension_semantics=("arbitrary",), vmem_limit_bytes=VMEM_LIMIT_BYTES),
        name="mix_router",
    )(x2d, yr2d, yd2d, w_top, w_bot, g2, rw_hi, rw_lo, rb)


EXPERT_TILE = 512
DISPATCH_TILE = 2048


def _dispatch_kernel(dest_ref, zero_ref, xp_ref, xs_ref, zbuf, sem, zsem):
    tm = xp_ref.shape[0] // PACK_CHUNKS
    blk_rows = EXPERT_TILE * PACK_CHUNKS

    @pl.when(pl.program_id(0) == 0)
    def _():
        zbuf[...] = jnp.zeros_like(zbuf)
        zero_block = lambda i: pltpu.make_async_copy(
            zbuf, xs_ref.at[pl.ds(pl.multiple_of(zero_ref[0, i] * blk_rows, blk_rows), blk_rows), :], zsem)
        for i in range(zero_ref.shape[1]):
            pl.when(zero_ref[0, i] >= 0)(lambda i=i: zero_block(i).start())
        for i in range(zero_ref.shape[1]):
            pl.when(zero_ref[0, i] >= 0)(lambda i=i: zero_block(i).wait())

    def issue(t, carry):
        src = _row_chunks(xp_ref, t, PACK_CHUNKS)
        for k in range(TOP_K):
            d = dest_ref[0, 0, t * TOP_K + k]
            pltpu.make_async_copy(src, _row_chunks(xs_ref, d, PACK_CHUNKS), sem).start(priority=k % 2)
        return carry

    lax.fori_loop(0, tm, issue, 0, unroll=8)
    for k in range(TOP_K):
        pltpu.make_async_copy(xp_ref, xs_ref.at[pl.ds(0, tm * PACK_CHUNKS), :], sem).wait()


def _dispatch(dest3d, zero_blocks, xp, n_slots):
    n_tok = xp.shape[0] // PACK_CHUNKS
    tm = DISPATCH_TILE
    return pl.pallas_call(
        _dispatch_kernel,
        grid=(n_tok // tm,),
        in_specs=[pl.BlockSpec((1, 1, tm * TOP_K), lambda i: (i, 0, 0), memory_space=pltpu.SMEM),
                  pl.BlockSpec(zero_blocks.shape, lambda i: (0, 0), memory_space=pltpu.SMEM),
                  pl.BlockSpec((tm * PACK_CHUNKS, LANES), lambda i: (i, 0))],
        out_specs=pl.BlockSpec(memory_space=pl.ANY),
        out_shape=jax.ShapeDtypeStruct((n_slots * PACK_CHUNKS, LANES), jnp.uint32),
        scratch_shapes=[pltpu.VMEM((EXPERT_TILE * PACK_CHUNKS, LANES), jnp.uint32),
                        pltpu.SemaphoreType.DMA(()), pltpu.SemaphoreType.DMA(())],
        compiler_params=pltpu.CompilerParams(
            dimension_semantics=("arbitrary",), vmem_limit_bytes=VMEM_LIMIT_BYTES),
        name="dispatch",
    )(dest3d, zero_blocks, xp)


SPLIT_BLOCK = 2 * LANES


def _expert_kernel(be_ref, src_ref, nu_ref, new_ref, xs_ref, wup_ref, wdn_ref, bg_ref, bl_ref, bd_ref,
                   y_ref, wg_s, wl_s, wd_s):
    del be_ref, src_ref
    j = pl.program_id(0)
    used = j < nu_ref[0]

    @pl.when(jnp.logical_not(used))
    def _():
        y_ref[...] = jnp.zeros_like(y_ref)

    @pl.when(jnp.logical_and(used, new_ref[j] == 1))
    def _():
        src = lax.broadcasted_iota(jnp.int32, (SPLIT_BLOCK, SPLIT_BLOCK), 0)
        dst = lax.broadcasted_iota(jnp.int32, (SPLIT_BLOCK, SPLIT_BLOCK), 1)
        perm = (src == jnp.where(dst < LANES, 2 * dst, 2 * (dst - LANES) + 1)).astype(BF16)
        for c in range(2 * D_FF // SPLIT_BLOCK):
            blk = wup_ref[0, :, c * SPLIT_BLOCK:(c + 1) * SPLIT_BLOCK].astype(BF16)
            out = _dot(blk, perm).astype(BF16)
            wg_s[:, c * LANES:(c + 1) * LANES] = out[:, :LANES]
            wl_s[:, c * LANES:(c + 1) * LANES] = out[:, LANES:]
        wd_s[...] = wdn_ref[0].astype(BF16)

    @pl.when(used)
    def _():
        x = _unpack_bf16_pairs(_load_row_chunks(xs_ref, EXPERT_TILE, PACK_CHUNKS))
        glu = jnp.minimum(_dot(x, wg_s[...]) + bg_ref[0], SWIGLU_LIMIT)
        lin = jnp.clip(_dot(x, wl_s[...]) + bl_ref[0], -SWIGLU_LIMIT, SWIGLU_LIMIT)
        act = glu * jax.nn.sigmoid(SWIGLU_ALPHA * glu) * (lin + 1.0)
        _store_row_chunks(y_ref, _dot(act.astype(BF16), wd_s[...]) + bd_ref[0])


def _experts(blk_expert, blk_src, n_used, blk_new, xs, w_up, w_down, bg, bl, bd):
    n_slots = xs.shape[0] // PACK_CHUNKS
    tb = EXPERT_TILE
    wspec = lambda k, n: pl.BlockSpec((1, k, n), lambda j, be, src, nu, new: (be[j], 0, 0))
    return pl.pallas_call(
        _expert_kernel,
        grid_spec=pltpu.PrefetchScalarGridSpec(
            num_scalar_prefetch=4,
            grid=(n_slots // tb,),
            in_specs=[pl.BlockSpec((tb * PACK_CHUNKS, LANES), lambda j, be, src, nu, new: (src[j], 0)),
                      wspec(D_MODEL, 2 * D_FF), wspec(D_FF, D_MODEL),
                      wspec(1, D_FF), wspec(1, D_FF), wspec(1, D_MODEL)],
            out_specs=pl.BlockSpec((tb * OUT_CHUNKS, LANES), lambda j, be, src, nu, new: (j, 0)),
            scratch_shapes=[pltpu.VMEM((D_MODEL, D_FF), BF16), pltpu.VMEM((D_MODEL, D_FF), BF16),
                            pltpu.VMEM((D_FF, D_MODEL), BF16)],
        ),
        out_shape=jax.ShapeDtypeStruct((n_slots * OUT_CHUNKS, LANES), F32),
        compiler_params=pltpu.CompilerParams(
            dimension_semantics=("arbitrary",), vmem_limit_bytes=VMEM_LIMIT_BYTES),
        name="experts",
    )(blk_expert, blk_src, n_used, blk_new, xs, w_up, w_down, bg, bl, bd)


COMBINE_TILE = 1024


def _combine_kernel(dest_ref, h_ref, gt_ref, g_ref, ys_ref, o_ref, buf, sem):
    tm = h_ref.shape[0]

    def issue(t, carry):
        for k in range(TOP_K):
            d = dest_ref[0, 0, t * TOP_K + k]
            pltpu.make_async_copy(_row_chunks(ys_ref, d, OUT_CHUNKS),
                                  _row_chunks(buf, k * tm + t, OUT_CHUNKS), sem).start(priority=k % 2)
        return carry

    lax.fori_loop(0, tm, issue, 0, unroll=8)
    rows_per_k = tm * OUT_CHUNKS
    for k in range(TOP_K):
        pltpu.make_async_copy(ys_ref.at[pl.ds(0, rows_per_k), :],
                              buf.at[pl.ds(k * rows_per_k, rows_per_k), :], sem).wait()
    gates = gt_ref[...]
    out = h_ref[...]
    for k in range(TOP_K):
        out = out + gates[:, k:k + 1] * _load_row_chunks(buf, tm, OUT_CHUNKS, first_row=k * tm)
    o_ref[...] = _rms(out, g_ref[...])


def _combine(dest3d, h, gates, final_g, ys):
    n_tok = h.shape[0]
    tm = COMBINE_TILE
    return pl.pallas_call(
        _combine_kernel,
        grid=(n_tok // tm,),
        in_specs=[pl.BlockSpec((1, 1, tm * TOP_K), lambda i: (i, 0, 0), memory_space=pltpu.SMEM),
                  pl.BlockSpec((tm, D_MODEL), lambda i: (i, 0)),
                  pl.BlockSpec((tm, TOP_K), lambda i: (i, 0)),
                  pl.BlockSpec((1, D_MODEL), lambda i: (0, 0)),
                  pl.BlockSpec(memory_space=pl.ANY)],
        out_specs=pl.BlockSpec((tm, D_MODEL), lambda i: (i, 0)),
        out_shape=jax.ShapeDtypeStruct((n_tok, D_MODEL), F32),
        scratch_shapes=[pltpu.VMEM((TOP_K * tm * OUT_CHUNKS, LANES), F32), pltpu.SemaphoreType.DMA(())],
        compiler_params=pltpu.CompilerParams(
            dimension_semantics=("arbitrary",), vmem_limit_bytes=VMEM_LIMIT_BYTES),
        name="combine",
    )(dest3d, h, gates, final_g, ys)


def _moe_stage(x2d, yr2d, yd2d, p):
    n_tok = x2d.shape[0]
    row = lambda t: t.reshape(1, -1)
    w_out = p['w_out'][0].astype(BF16)
    rw = p['router_w'][0]
    rw_hi = rw.astype(BF16)
    rw_lo = (rw - rw_hi.astype(F32)).astype(BF16)
    h, xp, top_i, gates, rank, counts = _mix_router(
        x2d, yr2d, yd2d, w_out[:RWKV_WIDTH], w_out[RWKV_WIDTH:], row(p['norm2_g'][0]),
        rw_hi, rw_lo, row(p['router_b'][0]))

    tb = EXPERT_TILE
    n_blocks = (n_tok * TOP_K) // tb + N_EXPERTS
    counts = counts[0].astype(jnp.int32)
    padded = (counts + tb - 1) // tb * tb
    experts = jnp.arange(N_EXPERTS, dtype=jnp.int32)
    pad_end = jnp.sum(jnp.where(experts[None, :] <= experts[:, None], padded[None, :], 0), axis=1)
    pad_start = pad_end - padded
    dest = rank + jnp.sum(jnp.where(top_i[..., None] == experts, pad_start, 0), axis=-1)
    n_used = pad_end[-1] // tb
    blk_src = jnp.minimum(jnp.arange(n_blocks, dtype=jnp.int32), n_used - 1)
    blk_expert = jnp.minimum(
        jnp.sum((pad_end[None, :] <= (blk_src * tb)[:, None]).astype(jnp.int32), axis=1), N_EXPERTS - 1)

    last_blk = jnp.where(padded > 0, pad_end // tb - 1, -1)
    tail_blk = jnp.where(n_used + experts < n_blocks, n_used + experts, -1)
    zero_blocks = jnp.concatenate([last_blk, tail_blk]).astype(jnp.int32).reshape(1, 2 * N_EXPERTS)
    xs = _dispatch(dest.reshape(n_tok // DISPATCH_TILE, 1, DISPATCH_TILE * TOP_K), zero_blocks, xp,
                   n_blocks * tb)

    b_up = p['exp_b_up'][0]
    bg, bl = b_up[:, None, 0::2], b_up[:, None, 1::2]
    blk_new = jnp.concatenate([jnp.ones((1,), jnp.int32),
                               (blk_expert[1:] != blk_expert[:-1]).astype(jnp.int32)])
    ys = _experts(blk_expert, blk_src.astype(jnp.int32), n_used.reshape(1).astype(jnp.int32), blk_new, xs,
                  p['exp_w_up'][0], p['exp_w_down'][0], bg, bl, p['exp_b_down'][0][:, None, :])
    return _combine(dest.reshape(n_tok // COMBINE_TILE, 1, COMBINE_TILE * TOP_K), h, gates,
                    row(p['final_g']), ys)


def _moe_stage_test(h2d, p):
    zeros = jnp.zeros((h2d.shape[0], RWKV_WIDTH), F32)
    return _moe_stage(h2d, zeros, zeros, p)


def kernel(x, norm1_g, w_in, rwkv_mu, rwkv_w0, rwkv_wb, rwkv_a0, rwkv_ab, rwkv_gb, rwkv_kk, rwkv_ka, rwkv_rk, rwkv_ln_g, rwkv_ln_b, diff_lq1, diff_lk1, diff_lq2, diff_lk2, diff_subln_g, w_out, norm2_g, router_w, router_b, exp_w_up, exp_b_up, exp_w_down, exp_b_down, final_g):
    p = dict(rwkv_mu=rwkv_mu, rwkv_w0=rwkv_w0, rwkv_wb=rwkv_wb, rwkv_a0=rwkv_a0, rwkv_ab=rwkv_ab,
             rwkv_gb=rwkv_gb, rwkv_kk=rwkv_kk, rwkv_ka=rwkv_ka, rwkv_rk=rwkv_rk,
             rwkv_ln_g=rwkv_ln_g, rwkv_ln_b=rwkv_ln_b, diff_lq1=diff_lq1, diff_lk1=diff_lk1,
             diff_lq2=diff_lq2, diff_lk2=diff_lk2, diff_subln_g=diff_subln_g, w_out=w_out,
             norm2_g=norm2_g, router_w=router_w, router_b=router_b, exp_w_up=exp_w_up,
             exp_b_up=exp_b_up, exp_w_down=exp_w_down, exp_b_down=exp_b_down, final_g=final_g)
    bsz, seq, d = x.shape
    n_tok = bsz * seq
    x2d = x.reshape(n_tok, d)
    proj = _inproj(x2d, norm1_g, w_in[0].astype(BF16)).reshape(bsz, seq, IN_COLS)
    y_rwkv = _rwkv_stage(proj, p).reshape(n_tok, RWKV_WIDTH)
    y_diff = _diff_stage(proj, p).reshape(n_tok, DIFF_WIDTH)
    return _moe_stage(x2d, y_rwkv, y_diff, p).reshape(bsz, seq, d)
```

```python
import math

import jax
import jax.numpy as jnp
from jax import lax
from jax.experimental import pallas as pl
from jax.experimental.pallas import tpu as pltpu

F32 = jnp.float32
BF16 = jnp.bfloat16

D_MODEL = 1024
RWKV_WIDTH = 512
RWKV_HEAD = 64
DECAY_LORA = 64
ICL_LORA = 64
GATE_LORA = 128
GN_EPS = 64e-5
DIFF_WIDTH = 512
DIFF_HEAD = 64
DIFF_HEADS = 4
DIFF_VDIM = 128
RWKV_COLS = 3 * RWKV_WIDTH + DECAY_LORA + ICL_LORA + GATE_LORA
IN_COLS = RWKV_COLS + 3 * DIFF_WIDTH
N_EXPERTS = 32
TOP_K = 4
D_FF = 1024
SWIGLU_ALPHA = 1.702
SWIGLU_LIMIT = 7.0
NORM_EPS = 1e-5
LAMBDA_INIT = 0.8 - 0.6 * math.exp(-0.0)

LANES = 128
VMEM_LIMIT_BYTES = 56 * 1024 * 1024

ROW_TILE = 512
RWKV_CHUNK = 64
RWKV_GROUP = 4
GROUP_LANES = RWKV_GROUP * RWKV_HEAD
PREP_UNROLL = 2


def _dot(a, b):
    return jnp.dot(a, b, preferred_element_type=F32)


def _dot_nt(a, b):
    return lax.dot_general(a, b, (((1,), (1,)), ((), ())), preferred_element_type=F32)


def _dot_tn(a, b):
    return lax.dot_general(a, b, (((0,), (0,)), ((), ())), preferred_element_type=F32)


def _split2(x):
    hi = x.astype(BF16)
    lo = (x - hi.astype(F32)).astype(BF16)
    return hi, lo


def _interleave(*stages):
    live = list(stages)
    while live:
        for gen in list(live):
            if next(gen, StopIteration) is StopIteration:
                live.remove(gen)


def _rms(x, g):
    return x * lax.rsqrt(jnp.mean(x * x, axis=-1, keepdims=True) + NORM_EPS) * g


def _inproj_kernel(x_ref, g_ref, w_ref, o_ref):
    xn = _rms(x_ref[...], g_ref[...])
    o_ref[...] = _dot(xn.astype(BF16), w_ref[...])


def _inproj(x2d, g, w_bf16):
    n_tok = x2d.shape[0]
    return pl.pallas_call(
        _inproj_kernel,
        grid=(n_tok // ROW_TILE,),
        in_specs=[
            pl.BlockSpec((ROW_TILE, D_MODEL), lambda i: (i, 0)),
            pl.BlockSpec((1, D_MODEL), lambda i: (0, 0)),
            pl.BlockSpec((D_MODEL, IN_COLS), lambda i: (0, 0)),
        ],
        out_specs=pl.BlockSpec((ROW_TILE, IN_COLS), lambda i: (i, 0)),
        out_shape=jax.ShapeDtypeStruct((n_tok, IN_COLS), F32),
        compiler_params=pltpu.CompilerParams(
            dimension_semantics=("arbitrary",), vmem_limit_bytes=VMEM_LIMIT_BYTES),
        name="inproj",
    )(x2d, g, w_bf16)


def _head_sum(x, bd_ones):
    xb = x.astype(BF16)
    return jnp.concatenate(
        [_dot(xb[:, g * GROUP_LANES:(g + 1) * GROUP_LANES], bd_ones)
         for g in range(RWKV_WIDTH // GROUP_LANES)], axis=1)


def _rwkv_kernel(p_ref, mu_ref, w0_ref, wb_ref, a0_ref, ab_ref, gb_ref, kkw_ref, ka_ref,
                 rk_ref, lng_ref, lnb_ref, o_ref,
                 carry_ref, state_ref, r_s, k_s, v_s, kk_s, b_s, lw_s, y_s, gate_s,
                 lhs_s, w0_s, yloc_s, arb_s, kb_s, dec_s):
    ts = p_ref.shape[1]
    n_chunks = ts // RWKV_CHUNK
    n_groups = RWKV_WIDTH // GROUP_LANES

    @pl.when(pl.program_id(1) == 0)
    def _():
        carry_ref[...] = jnp.zeros_like(carry_ref)
        state_ref[...] = jnp.zeros_like(state_ref)

    gi = lax.broadcasted_iota(jnp.int32, (GROUP_LANES, GROUP_LANES), 0)
    gj = lax.broadcasted_iota(jnp.int32, (GROUP_LANES, GROUP_LANES), 1)
    same_head = (gi // RWKV_HEAD) == (gj // RWKV_HEAD)
    strict = same_head & (gi % RWKV_CHUNK > gj % RWKV_CHUNK)
    incl = same_head & (gi % RWKV_CHUNK >= gj % RWKV_CHUNK)
    eye = (gi == gj).astype(F32)
    bd_ones = same_head.astype(BF16)

    pair_rows = PREP_UNROLL * RWKV_CHUNK
    first_row = lax.broadcasted_iota(jnp.int32, (pair_rows, 1), 0) == 0
    w1, w2, w3 = RWKV_WIDTH, 2 * RWKV_WIDTH, 3 * RWKV_WIDTH

    def project(cp):
        r0 = cp * pair_rows
        rows = slice(r0, r0 + pair_rows)

        def shifted(c0, c1):
            x = p_ref[0, rows, c0:c1]
            before = carry_ref[:, c0:c1] if cp == 0 else p_ref[0, r0 - 1:r0, c0:c1]
            prev = jnp.where(first_row, before, pltpu.roll(x, 1, 0))
            return x + (prev - x) * mu_ref[:, c0:c1]

        r = shifted(0, w1)
        k = shifted(w1, w2)
        v = shifted(w2, w3)
        yield
        wa_lo = shifted(w3, w3 + DECAY_LORA + ICL_LORA)
        g_lo = shifted(w3 + DECAY_LORA + ICL_LORA, RWKV_COLS)
        z = w0_ref[...] + _dot(jnp.tanh(wa_lo).astype(BF16), wb_ref[...])
        nz = -z
        softplus = jnp.maximum(nz, 0.0) + jnp.log1p(jnp.exp(-jnp.abs(nz)))
        lw_s[rows] = -jnp.exp(-softplus - 0.5)
        yield
        a = jax.nn.sigmoid(a0_ref[...] + _dot(wa_lo.astype(BF16), ab_ref[...]))
        gate_s[rows] = _dot(jax.nn.sigmoid(g_lo).astype(BF16), gb_ref[...])
        yield
        kk = k * kkw_ref[...]
        kk = kk * lax.rsqrt(jnp.maximum(_head_sum(kk * kk, bd_ones), 1e-24))
        r_s[rows] = r
        k_s[rows] = k * (1.0 + (a - 1.0) * ka_ref[...])
        v_s[rows] = v
        kk_s[rows] = kk
        b_s[rows] = kk * a

    def finish(cp):
        rows = slice(cp * pair_rows, (cp + 1) * pair_rows)
        y = y_s[rows]
        inv_n = 1.0 / RWKV_HEAD
        d = y - _head_sum(y, bd_ones) * inv_n
        yield
        var = _head_sum(d * d, bd_ones) * inv_n
        yn = d * lax.rsqrt(var + GN_EPS) * lng_ref[...] + lnb_ref[...]
        yield
        bonus = _head_sum(r_s[rows] * k_s[rows] * rk_ref[...], bd_ones) * v_s[rows]
        o_ref[0, rows] = (yn + bonus) * gate_s[rows]

    ci = lax.broadcasted_iota(jnp.int32, (RWKV_CHUNK, RWKV_CHUNK), 0)
    cj = lax.broadcasted_iota(jnp.int32, (RWKV_CHUNK, RWKV_CHUNK), 1)
    tri = (ci >= cj).astype(BF16)
    tri2 = jnp.concatenate([tri, tri], axis=1)

    def tile_heads(x):
        return jnp.concatenate([x] * RWKV_GROUP, axis=0)

    def fold_heads(x):
        out = x[0:RWKV_CHUNK]
        for i in range(1, RWKV_GROUP):
            out = out + x[i * RWKV_CHUNK:(i + 1) * RWKV_CHUNK]
        return out

    def chunk_rows(c):
        return slice(c * RWKV_CHUNK, (c + 1) * RWKV_CHUNK)

    def prepare(cp):
        units = [(cp * PREP_UNROLL + i, g) for i in range(PREP_UNROLL) for g in range(n_groups)]
        lanes = [slice(g * GROUP_LANES, (g + 1) * GROUP_LANES) for _, g in units]
        rows = [chunk_rows(c) for c, _ in units]
        load = lambda ref: [ref[rw, ln] for rw, ln in zip(rows, lanes)]
        rc, kc, vc, kkc, bc, lwc = (load(ref) for ref in (r_s, k_s, v_s, kk_s, b_s, lw_s))
        cum = [_dot(tri2, jnp.concatenate(_split2(x), axis=0)) for x in lwc]
        yield
        cum_end = [x[RWKV_CHUNK - 1:RWKV_CHUNK, :] for x in cum]
        for (c, _), ln, ce in zip(units, lanes, cum_end):
            dec_s[c, :, ln] = jnp.exp(ce)
        e_neg = [jnp.exp(-x) for x in cum]
        e_end = [jnp.exp(ce - x) for ce, x in zip(cum_end, cum)]
        kkd = [a * jnp.exp(x - lw) for a, x, lw in zip(kkc, cum, lwc)]
        rd = [a * jnp.exp(x) for a, x in zip(rc, cum)]
        spread = lambda xs: [jnp.where(same_head, tile_heads(x), 0.0).astype(BF16) for x in xs]
        kkd_s, rd_s, v_st = spread(kkd), spread(rd), spread(vc)
        rhs = [jnp.concatenate([tile_heads(b * e), tile_heads(k * e)], axis=0).astype(BF16)
               for b, k, e in zip(bc, kc, e_neg)]
        yield
        gram = [_dot_nt(jnp.concatenate([a, b], axis=0), w) for a, b, w in zip(kkd_s, rd_s, rhs)]
        yield
        gl = GROUP_LANES
        a_ab = [jnp.where(strict, x[:gl, :gl], 0.0) for x in gram]
        a_rb = [jnp.where(incl, x[gl:, :gl], 0.0).astype(BF16) for x in gram]
        a_ak = [jnp.where(strict, x[:gl, gl:], 0.0).astype(BF16) for x in gram]
        a_rk = [jnp.where(incl, x[gl:, gl:], 0.0).astype(BF16) for x in gram]
        av = [_dot(jnp.concatenate([a, b], axis=0), v) for a, b, v in zip(a_ak, a_rk, v_st)]
        yield

        pw = [x.astype(BF16) for x in a_ab]
        t_inv = [eye - x for x in a_ab]
        for _ in range(int(math.log2(RWKV_CHUNK)) - 1):
            pw = [_dot(x, x).astype(BF16) for x in pw]
            t_inv = [t + _dot(t.astype(BF16), x) for t, x in zip(t_inv, pw)]
            yield
        sol = [_dot(t.astype(BF16), jnp.concatenate([a, x[:gl].astype(BF16)], axis=1))
               for t, a, x in zip(t_inv, kkd_s, av)]
        yield
        for i, (c, g) in enumerate(units):
            lhs_s[c, g] = jnp.concatenate([sol[i][:, :gl].astype(BF16), rd_s[i]], axis=0)
            w0_s[c, g] = sol[i][:, gl:]
            yloc_s[c, g] = fold_heads(av[i][gl:])
            arb_s[c, g] = a_rb[i]
            kb_s[c, g] = jnp.concatenate([kc[i] * e_end[i], bc[i] * e_end[i]], axis=0).astype(BF16)

    def advance(c):
        rows = chunk_rows(c)
        gs = range(n_groups)
        lanes = [slice(g * GROUP_LANES, (g + 1) * GROUP_LANES) for g in gs]
        state = [state_ref[g] for g in gs]
        m1 = [_dot_nt(lhs_s[c, g], state[g].astype(BF16)) for g in gs]
        yield
        sa_st = [m1[g][:GROUP_LANES] + w0_s[c, g] for g in gs]
        y_st = [m1[g][GROUP_LANES:] - _dot(arb_s[c, g], sa_st[g].astype(BF16)) for g in gs]
        yield
        upd = [_dot_tn(jnp.concatenate([v_s[rows, lanes[g]], -fold_heads(sa_st[g])], axis=0).astype(BF16),
                       kb_s[c, g]) for g in gs]
        yield
        for g in gs:
            y_s[rows, lanes[g]] = fold_heads(y_st[g]) + yloc_s[c, g]
            state_ref[g] = state[g] * dec_s[c, :, lanes[g]] + jnp.where(same_head, upd[g], 0.0)
        yield

    def advance_chunks(cp):
        for i in range(PREP_UNROLL):
            yield from advance(cp * PREP_UNROLL + i)

    n_pairs = n_chunks // PREP_UNROLL
    _interleave(project(0))
    _interleave(prepare(0), *([project(1)] if n_pairs > 1 else []))
    for cp in range(n_pairs):
        stages = [advance_chunks(cp)]
        if cp + 1 < n_pairs:
            stages.append(prepare(cp + 1))
        if cp + 2 < n_pairs:
            stages.append(project(cp + 2))
        if cp >= 1:
            stages.append(finish(cp - 1))
        _interleave(*stages)
    _interleave(finish(n_pairs - 1))
    carry_ref[...] = p_ref[0, ts - 1:ts, :]


def _rwkv(proj3d, mu, w0, wb_pad, a0, ab_pad, gb, kkw, ka, rk, lng, lnb):
    bsz, seq, _ = proj3d.shape
    ts = ROW_TILE
    vec = lambda n: pl.BlockSpec((1, n), lambda b, s: (0, 0))
    mat = lambda m, n: pl.BlockSpec((m, n), lambda b, s: (0, 0))
    tile_f32 = pltpu.VMEM((ts, RWKV_WIDTH), F32)
    n_chunks = ts // RWKV_CHUNK
    n_groups = RWKV_WIDTH // GROUP_LANES
    return pl.pallas_call(
        _rwkv_kernel,
        grid=(bsz, seq // ts),
        in_specs=[
            pl.BlockSpec((1, ts, RWKV_COLS), lambda b, s: (b, s, 0)),
            vec(RWKV_COLS), vec(RWKV_WIDTH), mat(DECAY_LORA + ICL_LORA, RWKV_WIDTH),
            vec(RWKV_WIDTH), mat(DECAY_LORA + ICL_LORA, RWKV_WIDTH), mat(GATE_LORA, RWKV_WIDTH),
            vec(RWKV_WIDTH), vec(RWKV_WIDTH), vec(RWKV_WIDTH), vec(RWKV_WIDTH), vec(RWKV_WIDTH),
        ],
        out_specs=pl.BlockSpec((1, ts, RWKV_WIDTH), lambda b, s: (b, s, 0)),
        out_shape=jax.ShapeDtypeStruct((bsz, seq, RWKV_WIDTH), F32),
        scratch_shapes=[
            pltpu.VMEM((1, RWKV_COLS), F32),
            pltpu.VMEM((n_groups, GROUP_LANES, GROUP_LANES), F32),
            tile_f32, tile_f32, tile_f32, tile_f32, tile_f32, tile_f32, tile_f32, tile_f32,
            pltpu.VMEM((n_chunks, n_groups, 2 * GROUP_LANES, GROUP_LANES), BF16),
            pltpu.VMEM((n_chunks, n_groups, GROUP_LANES, GROUP_LANES), F32),
            pltpu.VMEM((n_chunks, n_groups, RWKV_CHUNK, GROUP_LANES), F32),
            pltpu.VMEM((n_chunks, n_groups, GROUP_LANES, GROUP_LANES), BF16),
            pltpu.VMEM((n_chunks, n_groups, 2 * RWKV_CHUNK, GROUP_LANES), BF16),
            pltpu.VMEM((n_chunks, 1, RWKV_WIDTH), F32),
        ],
        compiler_params=pltpu.CompilerParams(
            dimension_semantics=("arbitrary", "arbitrary"), vmem_limit_bytes=VMEM_LIMIT_BYTES),
        name="rwkv7",
    )(proj3d, mu, w0, wb_pad, a0, ab_pad, gb, kkw, ka, rk, lng, lnb)


def _rwkv_stage(proj3d, p):
    row = lambda t: t.reshape(1, -1)
    zeros = jnp.zeros((DECAY_LORA, RWKV_WIDTH), F32)
    wb_pad = jnp.concatenate([p['rwkv_wb'][0], zeros], axis=0).astype(BF16)
    ab_pad = jnp.concatenate([zeros, p['rwkv_ab'][0]], axis=0).astype(BF16)
    return _rwkv(proj3d, row(p['rwkv_mu'][0]), row(p['rwkv_w0'][0]), wb_pad, row(p['rwkv_a0'][0]), ab_pad,
                 p['rwkv_gb'][0].astype(BF16), row(p['rwkv_kk'][0]), row(p['rwkv_ka'][0]),
                 row(p['rwkv_rk'][0]), row(p['rwkv_ln_g'][0]), row(p['rwkv_ln_b'][0]))


ATT_TILE = 512
ATT_LOCKSTEP = 2
ATT_AUG = 2 * LANES
POS_SPLIT_BITS = 6
MASK_VALUE = -1e30


def _alibi_lanes(n, pos0, slope, key_side):
    pos = pos0 + lax.broadcasted_iota(jnp.int32, (n, LANES), 0)
    lane = lax.broadcasted_iota(jnp.int32, (n, LANES), 1)
    hi = (pos >> POS_SPLIT_BITS).astype(F32) * (slope * float(1 << POS_SPLIT_BITS))
    lo = (pos & ((1 << POS_SPLIT_BITS) - 1)).astype(F32) * slope
    if key_side:
        return jnp.where(lane == 0, hi, jnp.where(lane == 1, lo, jnp.where(lane < 4, 1.0, 0.0)))
    return jnp.where(lane < 2, 1.0, jnp.where(lane == 2, -hi, jnp.where(lane == 3, -lo, 0.0)))


def _diff_kernel(slope_ref, lq1_ref, lk1_ref, lq2_ref, lk2_ref, g_ref, *refs):
    nh = DIFF_HEADS
    q_refs, k_refs, v_refs = refs[:nh], refs[nh:2 * nh], refs[2 * nh:3 * nh]
    o_ref, ks_ref, vs_ref, m_ref, acc_ref = refs[3 * nh:]
    tq = q_refs[0].shape[1]
    seq = k_refs[0].shape[1]
    qi = pl.program_id(1)
    slopes = [slope_ref[h][:, :1] for h in range(nh)]

    @pl.when(qi == 0)
    def _():
        for h in range(nh):
            ks_ref[h, :, :LANES] = k_refs[h][0].astype(BF16)
            ks_ref[h, :, LANES:] = _alibi_lanes(seq, 0, slopes[h], True).astype(BF16)
            vs_ref[h, :, :LANES] = v_refs[h][0].astype(BF16)
            vs_ref[h, :, LANES:] = jnp.ones((seq, LANES), BF16)

    lane = lax.broadcasted_iota(jnp.int32, (1, LANES), 1)
    qa = []
    for h in range(nh):
        q = q_refs[h][0] * (DIFF_HEAD ** -0.5)
        q_terms = _alibi_lanes(tq, qi * tq, slopes[h], False)
        qa.append(jnp.concatenate(
            [jnp.concatenate([jnp.where(lane < DIFF_HEAD, q, 0.0), q_terms], axis=1),
             jnp.concatenate([jnp.where(lane >= DIFF_HEAD, q, 0.0), q_terms], axis=1)],
            axis=0).astype(BF16))
    m_ref[...] = jnp.full_like(m_ref, MASK_VALUE)
    acc_ref[...] = jnp.zeros_like(acc_ref)
    causal = (lax.broadcasted_iota(jnp.int32, (tq, tq), 0)
              >= lax.broadcasted_iota(jnp.int32, (tq, tq), 1))
    causal = jnp.concatenate([causal, causal], axis=0)

    def process(j, diagonal):
        rows = pl.ds(pl.multiple_of(j * tq, tq), tq)
        for h0 in range(0, nh, ATT_LOCKSTEP):
            hs = range(h0, h0 + ATT_LOCKSTEP)
            s = {h: _dot_nt(qa[h], ks_ref[h, rows, :]) for h in hs}
            if diagonal:
                s = {h: jnp.where(causal, s[h], MASK_VALUE) for h in hs}
            m_old = {h: m_ref[h] for h in hs}
            m_new = {h: jnp.maximum(m_old[h], jnp.max(s[h], axis=-1, keepdims=True)) for h in hs}
            pr = {h: jnp.exp(s[h] - jnp.concatenate([m_new[h]] * (tq // LANES), axis=1)).astype(BF16)
                  for h in hs}
            pv = {h: _dot(pr[h], vs_ref[h, rows, :]) for h in hs}
            for h in hs:
                alpha = jnp.exp(m_old[h] - m_new[h])
                acc_ref[h] = jnp.concatenate([alpha, alpha], axis=1) * acc_ref[h] + pv[h]
                m_ref[h] = m_new[h]

    def off_diagonal(j, carry):
        process(j, False)
        return carry

    lax.fori_loop(0, qi, off_diagonal, 0)
    process(qi, True)

    lam = (jnp.exp(jnp.sum(lq1_ref[...] * lk1_ref[...], axis=-1, keepdims=True))
           - jnp.exp(jnp.sum(lq2_ref[...] * lk2_ref[...], axis=-1, keepdims=True)) + LAMBDA_INIT)
    for h in range(nh):
        acc = acc_ref[h]
        o12 = acc[:, :DIFF_VDIM] / acc[:, DIFF_VDIM:]
        o = o12[:tq] - lam * o12[tq:]
        o_ref[0, :, h * DIFF_VDIM:(h + 1) * DIFF_VDIM] = _rms(o, g_ref[...]) * (1.0 - LAMBDA_INIT)


def _diff(proj3d, slopes, lq1, lk1, lq2, lk2, subln_g):
    bsz, seq, _ = proj3d.shape
    tq = ATT_TILE
    q_blk0 = RWKV_COLS // DIFF_VDIM
    k_blk0 = q_blk0 + DIFF_HEADS
    v_blk0 = k_blk0 + DIFF_HEADS
    nh = DIFF_HEADS
    vec = lambda n: pl.BlockSpec((1, n), lambda b, i: (0, 0))
    q_spec = lambda h: pl.BlockSpec((1, tq, DIFF_VDIM), lambda b, i: (b, i, q_blk0 + h))
    kv_spec = lambda blk: pl.BlockSpec((1, seq, DIFF_VDIM), lambda b, i: (b, 0, blk))
    return pl.pallas_call(
        _diff_kernel,
        grid=(bsz, seq // tq),
        in_specs=([pl.BlockSpec((nh, 1, LANES), lambda b, i: (0, 0, 0)),
                   vec(DIFF_HEAD), vec(DIFF_HEAD), vec(DIFF_HEAD), vec(DIFF_HEAD), vec(DIFF_VDIM)]
                  + [q_spec(h) for h in range(nh)]
                  + [kv_spec(k_blk0 + h) for h in range(nh)]
                  + [kv_spec(v_blk0 + h) for h in range(nh)]),
        out_specs=pl.BlockSpec((1, tq, DIFF_WIDTH), lambda b, i: (b, i, 0)),
        out_shape=jax.ShapeDtypeStruct((bsz, seq, DIFF_WIDTH), F32),
        scratch_shapes=[pltpu.VMEM((nh, seq, ATT_AUG), BF16), pltpu.VMEM((nh, seq, ATT_AUG), BF16),
                        pltpu.VMEM((nh, 2 * tq, LANES), F32), pltpu.VMEM((nh, 2 * tq, ATT_AUG), F32)],
        compiler_params=pltpu.CompilerParams(
            dimension_semantics=("arbitrary", "arbitrary"), vmem_limit_bytes=VMEM_LIMIT_BYTES),
        name="diffattn",
    )(slopes, lq1, lk1, lq2, lk2, subln_g, *([proj3d] * (3 * nh)))


def _diff_stage(proj3d, p):
    row = lambda t: t.reshape(1, -1)
    slopes = 2.0 ** (-8.0 * (jnp.arange(DIFF_HEADS, dtype=F32) + 1.0) / DIFF_HEADS)
    slopes = jnp.broadcast_to(slopes[:, None, None], (DIFF_HEADS, 1, LANES))
    return _diff(proj3d, slopes, row(p['diff_lq1'][0]), row(p['diff_lk1'][0]), row(p['diff_lq2'][0]),
                 row(p['diff_lk2'][0]), row(p['diff_subln_g'][0]))


PACK_COLS = D_MODEL // 2
ROUTER_LOCKSTEP = 2
PACK_CHUNKS = PACK_COLS // LANES
OUT_CHUNKS = D_MODEL // LANES


def _pack_bf16_pairs(x):
    xb = x.astype(BF16).astype(F32)
    lo = lax.bitcast_convert_type(xb[:, :PACK_COLS], jnp.uint32) >> 16
    hi = lax.bitcast_convert_type(xb[:, PACK_COLS:], jnp.uint32)
    return lo | hi


def _unpack_bf16_pairs(xp):
    lo = lax.bitcast_convert_type(xp << 16, F32)
    hi = lax.bitcast_convert_type(xp & jnp.uint32(0xFFFF0000), F32)
    return jnp.concatenate([lo, hi], axis=1).astype(BF16)


def _store_row_chunks(ref, value, first_row=0):
    n_rows, width = value.shape
    n_chunks = width // LANES
    for c in range(n_chunks):
        ref[pl.ds(first_row * n_chunks + c, n_rows, stride=n_chunks), :] = value[:, c * LANES:(c + 1) * LANES]


def _load_row_chunks(ref, n_rows, n_chunks, first_row=0):
    return jnp.concatenate(
        [ref[pl.ds(first_row * n_chunks + c, n_rows, stride=n_chunks), :] for c in range(n_chunks)], axis=1)


def _row_chunks(ref, row, n_chunks):
    return ref.at[pl.ds(pl.multiple_of(row * n_chunks, n_chunks), n_chunks), :]


def _mix_router_kernel(x_ref, yr_ref, yd_ref, wt_ref, wb_ref, g_ref, rwh_ref, rwl_ref, rb_ref,
                       h_ref, xp_ref, ti_ref, gt_ref, rk_ref, cnt_ref):
    tm = x_ref.shape[0]
    sub = tm // ROUTER_LOCKSTEP

    @pl.when(pl.program_id(0) == 0)
    def _():
        cnt_ref[...] = jnp.zeros_like(cnt_ref)

    lane_e = lax.broadcasted_iota(jnp.int32, (sub, N_EXPERTS), 1)
    lane_k = lax.broadcasted_iota(jnp.int32, (sub, TOP_K), 1)
    ri = lax.broadcasted_iota(jnp.int32, (sub, sub), 0)
    rj = lax.broadcasted_iota(jnp.int32, (sub, sub), 1)
    before = (rj < ri).astype(BF16)
    seen = [cnt_ref[...]]

    def route(i):
        rows = slice(i * sub, (i + 1) * sub)
        h = (x_ref[rows] + _dot(yr_ref[rows].astype(BF16), wt_ref[...])
             + _dot(yd_ref[rows].astype(BF16), wb_ref[...]))
        h_ref[rows] = h
        yield
        xn = _rms(h, g_ref[...])
        _store_row_chunks(xp_ref, _pack_bf16_pairs(xn), first_row=i * sub)
        x_hi, x_lo = _split2(xn)
        vals = (_dot(x_hi, rwh_ref[...]) + _dot(x_hi, rwl_ref[...]) + _dot(x_lo, rwh_ref[...])
                + rb_ref[...])
        yield
        tops, sels = [], []
        top_i = jnp.zeros((sub, TOP_K), jnp.int32)
        for k in range(TOP_K):
            m = jnp.max(vals, axis=-1, keepdims=True)
            idx = jnp.min(jnp.where(vals == m, lane_e, N_EXPERTS), axis=-1, keepdims=True)
            sel = lane_e == idx
            vals = jnp.where(sel, -jnp.inf, vals)
            tops.append(m)
            sels.append(sel)
            top_i = jnp.where(lane_k == k, idx, top_i)
            yield
        ti_ref[rows] = top_i
        exps = [jnp.exp(m - tops[0]) for m in tops]
        den = exps[0] + exps[1] + exps[2] + exps[3]
        gates = jnp.zeros((sub, TOP_K), F32)
        for k in range(TOP_K):
            gates = jnp.where(lane_k == k, exps[k] / den, gates)
        gt_ref[rows] = gates
        cnt = jnp.zeros((sub, N_EXPERTS), F32)
        for sel in sels:
            cnt = cnt + sel.astype(F32)
        earlier = seen[0]
        for part in seen[1:i + 1]:
            earlier = earlier + part
        seen.append(jnp.sum(cnt, axis=0, keepdims=True))
        yield
        prefix = _dot(before, cnt.astype(BF16)) + earlier
        rank = jnp.zeros((sub, TOP_K), F32)
        for k in range(TOP_K):
            rk = jnp.sum(jnp.where(sels[k], prefix, 0.0), axis=-1, keepdims=True)
            rank = jnp.where(lane_k == k, rk, rank)
        rk_ref[rows] = rank.astype(jnp.int32)

    _interleave(*[route(i) for i in range(ROUTER_LOCKSTEP)])
    total = seen[0]
    for part in seen[1:]:
        total = total + part
    cnt_ref[...] = total


def _mix_router(x2d, yr2d, yd2d, w_top, w_bot, g2, rw_hi, rw_lo, rb):
    n_tok = x2d.shape[0]
    tm = ROW_TILE
    rows = lambda n: pl.BlockSpec((tm, n), lambda i: (i, 0))
    full = lambda m, n: pl.BlockSpec((m, n), lambda i: (0, 0))
    return pl.pallas_call(
        _mix_router_kernel,
        grid=(n_tok // tm,),
        in_specs=[rows(D_MODEL), rows(RWKV_WIDTH), rows(DIFF_WIDTH),
                  full(RWKV_WIDTH, D_MODEL), full(DIFF_WIDTH, D_MODEL), full(1, D_MODEL),
                  full(D_MODEL, N_EXPERTS), full(D_MODEL, N_EXPERTS), full(1, N_EXPERTS)],
        out_specs=[rows(D_MODEL), pl.BlockSpec((tm * PACK_CHUNKS, LANES), lambda i: (i, 0)),
                   rows(TOP_K), rows(TOP_K), rows(TOP_K),
                   full(1, N_EXPERTS)],
        out_shape=[jax.ShapeDtypeStruct((n_tok, D_MODEL), F32),
                   jax.ShapeDtypeStruct((n_tok * PACK_CHUNKS, LANES), jnp.uint32),
                   jax.ShapeDtypeStruct((n_tok, TOP_K), jnp.int32),
                   jax.ShapeDtypeStruct((n_tok, TOP_K), F32),
                   jax.ShapeDtypeStruct((n_tok, TOP_K), jnp.int32),
                   jax.ShapeDtypeStruct((1, N_EXPERTS), F32)],
        compiler_params=pltpu.CompilerParams(
            dimension_semantics=("arbitrary",), vmem_limit_bytes=VMEM_LIMIT_BYTES),
        name="mix_router",
    )(x2d, yr2d, yd2d, w_top, w_bot, g2, rw_hi, rw_lo, rb)


EXPERT_TILE = 512
DISPATCH_TILE = 4096


def _dispatch_kernel(dest_ref, zero_ref, xp_ref, xs_ref, zbuf, sem, zsem):
    tm = xp_ref.shape[0] // PACK_CHUNKS
    blk_rows = EXPERT_TILE * PACK_CHUNKS

    @pl.when(pl.program_id(0) == 0)
    def _():
        zbuf[...] = jnp.zeros_like(zbuf)
        zero_block = lambda i: pltpu.make_async_copy(
            zbuf, xs_ref.at[pl.ds(pl.multiple_of(zero_ref[0, i] * blk_rows, blk_rows), blk_rows), :], zsem)
        for i in range(zero_ref.shape[1]):
            pl.when(zero_ref[0, i] >= 0)(lambda i=i: zero_block(i).start())
        for i in range(zero_ref.shape[1]):
            pl.when(zero_ref[0, i] >= 0)(lambda i=i: zero_block(i).wait())

    def issue(t, carry):
        src = _row_chunks(xp_ref, t, PACK_CHUNKS)
        for k in range(TOP_K):
            d = dest_ref[0, 0, t * TOP_K + k]
            pltpu.make_async_copy(src, _row_chunks(xs_ref, d, PACK_CHUNKS), sem).start(priority=k % 2)
        return carry

    lax.fori_loop(0, tm, issue, 0, unroll=8)
    for k in range(TOP_K):
        pltpu.make_async_copy(xp_ref, xs_ref.at[pl.ds(0, tm * PACK_CHUNKS), :], sem).wait()


def _dispatch(dest3d, zero_blocks, xp, n_slots):
    n_tok = xp.shape[0] // PACK_CHUNKS
    tm = DISPATCH_TILE
    return pl.pallas_call(
        _dispatch_kernel,
        grid=(n_tok // tm,),
        in_specs=[pl.BlockSpec((1, 1, tm * TOP_K), lambda i: (i, 0, 0), memory_space=pltpu.SMEM),
                  pl.BlockSpec(zero_blocks.shape, lambda i: (0, 0), memory_space=pltpu.SMEM),
                  pl.BlockSpec((tm * PACK_CHUNKS, LANES), lambda i: (i, 0))],
        out_specs=pl.BlockSpec(memory_space=pl.ANY),
        out_shape=jax.ShapeDtypeStruct((n_slots * PACK_CHUNKS, LANES), jnp.uint32),
        scratch_shapes=[pltpu.VMEM((EXPERT_TILE * PACK_CHUNKS, LANES), jnp.uint32),
                        pltpu.SemaphoreType.DMA(()), pltpu.SemaphoreType.DMA(())],
        compiler_params=pltpu.CompilerParams(
            dimension_semantics=("arbitrary",), vmem_limit_bytes=VMEM_LIMIT_BYTES),
        name="dispatch",
    )(dest3d, zero_blocks, xp)


SPLIT_BLOCK = 2 * LANES


def _expert_kernel(be_ref, src_ref, nu_ref, new_ref, xs_ref, wup_ref, wdn_ref, bg_ref, bl_ref, bd_ref,
                   y_ref, wg_s, wl_s, wd_s):
    del be_ref, src_ref
    j = pl.program_id(0)
    used = j < nu_ref[0]

    @pl.when(jnp.logical_not(used))
    def _():
        y_ref[...] = jnp.zeros_like(y_ref)

    @pl.when(jnp.logical_and(used, new_ref[j] == 1))
    def _():
        src = lax.broadcasted_iota(jnp.int32, (SPLIT_BLOCK, SPLIT_BLOCK), 0)
        dst = lax.broadcasted_iota(jnp.int32, (SPLIT_BLOCK, SPLIT_BLOCK), 1)
        perm = (src == jnp.where(dst < LANES, 2 * dst, 2 * (dst - LANES) + 1)).astype(BF16)
        for c in range(2 * D_FF // SPLIT_BLOCK):
            blk = wup_ref[0, :, c * SPLIT_BLOCK:(c + 1) * SPLIT_BLOCK].astype(BF16)
            out = _dot(blk, perm).astype(BF16)
            wg_s[:, c * LANES:(c + 1) * LANES] = out[:, :LANES]
            wl_s[:, c * LANES:(c + 1) * LANES] = out[:, LANES:]
        wd_s[...] = wdn_ref[0].astype(BF16)

    @pl.when(used)
    def _():
        x = _unpack_bf16_pairs(_load_row_chunks(xs_ref, EXPERT_TILE, PACK_CHUNKS))
        glu = jnp.minimum(_dot(x, wg_s[...]) + bg_ref[0], SWIGLU_LIMIT)
        lin = jnp.clip(_dot(x, wl_s[...]) + bl_ref[0], -SWIGLU_LIMIT, SWIGLU_LIMIT)
        act = glu * jax.nn.sigmoid(SWIGLU_ALPHA * glu) * (lin + 1.0)
        _store_row_chunks(y_ref, _dot(act.astype(BF16), wd_s[...]) + bd_ref[0])


def _experts(blk_expert, blk_src, n_used, blk_new, xs, w_up, w_down, bg, bl, bd):
    n_slots = xs.shape[0] // PACK_CHUNKS
    tb = EXPERT_TILE
    wspec = lambda k, n: pl.BlockSpec((1, k, n), lambda j, be, src, nu, new: (be[j], 0, 0))
    return pl.pallas_call(
        _expert_kernel,
        grid_spec=pltpu.PrefetchScalarGridSpec(
            num_scalar_prefetch=4,
            grid=(n_slots // tb,),
            in_specs=[pl.BlockSpec((tb * PACK_CHUNKS, LANES), lambda j, be, src, nu, new: (src[j], 0)),
                      wspec(D_MODEL, 2 * D_FF), wspec(D_FF, D_MODEL),
                      wspec(1, D_FF), wspec(1, D_FF), wspec(1, D_MODEL)],
            out_specs=pl.BlockSpec((tb * OUT_CHUNKS, LANES), lambda j, be, src, nu, new: (j, 0)),
            scratch_shapes=[pltpu.VMEM((D_MODEL, D_FF), BF16), pltpu.VMEM((D_MODEL, D_FF), BF16),
                            pltpu.VMEM((D_FF, D_MODEL), BF16)],
        ),
        out_shape=jax.ShapeDtypeStruct((n_slots * OUT_CHUNKS, LANES), F32),
        compiler_params=pltpu.CompilerParams(
            dimension_semantics=("arbitrary",), vmem_limit_bytes=VMEM_LIMIT_BYTES),
        name="experts",
    )(blk_expert, blk_src, n_used, blk_new, xs, w_up, w_down, bg, bl, bd)


COMBINE_TILE = 1024


def _combine_kernel(dest_ref, h_ref, gt_ref, g_ref, ys_ref, o_ref, buf, sem):
    tm = h_ref.shape[0]

    def issue(t, carry):
        for k in range(TOP_K):
            d = dest_ref[0, 0, t * TOP_K + k]
            pltpu.make_async_copy(_row_chunks(ys_ref, d, OUT_CHUNKS),
                                  _row_chunks(buf, k * tm + t, OUT_CHUNKS), sem).start(priority=k % 2)
        return carry

    lax.fori_loop(0, tm, issue, 0, unroll=8)
    rows_per_k = tm * OUT_CHUNKS
    for k in range(TOP_K):
        pltpu.make_async_copy(ys_ref.at[pl.ds(0, rows_per_k), :],
                              buf.at[pl.ds(k * rows_per_k, rows_per_k), :], sem).wait()
    gates = gt_ref[...]
    out = h_ref[...]
    for k in range(TOP_K):
        out = out + gates[:, k:k + 1] * _load_row_chunks(buf, tm, OUT_CHUNKS, first_row=k * tm)
    o_ref[...] = _rms(out, g_ref[...])


def _combine(dest3d, h, gates, final_g, ys):
    n_tok = h.shape[0]
    tm = COMBINE_TILE
    return pl.pallas_call(
        _combine_kernel,
        grid=(n_tok // tm,),
        in_specs=[pl.BlockSpec((1, 1, tm * TOP_K), lambda i: (i, 0, 0), memory_space=pltpu.SMEM),
                  pl.BlockSpec((tm, D_MODEL), lambda i: (i, 0)),
                  pl.BlockSpec((tm, TOP_K), lambda i: (i, 0)),
                  pl.BlockSpec((1, D_MODEL), lambda i: (0, 0)),
                  pl.BlockSpec(memory_space=pl.ANY)],
        out_specs=pl.BlockSpec((tm, D_MODEL), lambda i: (i, 0)),
        out_shape=jax.ShapeDtypeStruct((n_tok, D_MODEL), F32),
        scratch_shapes=[pltpu.VMEM((TOP_K * tm * OUT_CHUNKS, LANES), F32), pltpu.SemaphoreType.DMA(())],
        compiler_params=pltpu.CompilerParams(
            dimension_semantics=("arbitrary",), vmem_limit_bytes=VMEM_LIMIT_BYTES),
        name="combine",
    )(dest3d, h, gates, final_g, ys)


def _moe_stage(x2d, yr2d, yd2d, p):
    n_tok = x2d.shape[0]
    row = lambda t: t.reshape(1, -1)
    w_out = p['w_out'][0].astype(BF16)
    rw = p['router_w'][0]
    rw_hi = rw.astype(BF16)
    rw_lo = (rw - rw_hi.astype(F32)).astype(BF16)
    h, xp, top_i, gates, rank, counts = _mix_router(
        x2d, yr2d, yd2d, w_out[:RWKV_WIDTH], w_out[RWKV_WIDTH:], row(p['norm2_g'][0]),
        rw_hi, rw_lo, row(p['router_b'][0]))

    tb = EXPERT_TILE
    n_blocks = (n_tok * TOP_K) // tb + N_EXPERTS
    counts = counts[0].astype(jnp.int32)
    padded = (counts + tb - 1) // tb * tb
    experts = jnp.arange(N_EXPERTS, dtype=jnp.int32)
    pad_end = jnp.sum(jnp.where(experts[None, :] <= experts[:, None], padded[None, :], 0), axis=1)
    pad_start = pad_end - padded
    dest = rank + jnp.sum(jnp.where(top_i[..., None] == experts, pad_start, 0), axis=-1)
    n_used = pad_end[-1] // tb
    blk_src = jnp.minimum(jnp.arange(n_blocks, dtype=jnp.int32), n_used - 1)
    blk_expert = jnp.minimum(
        jnp.sum((pad_end[None, :] <= (blk_src * tb)[:, None]).astype(jnp.int32), axis=1), N_EXPERTS - 1)

    last_blk = jnp.where(padded > 0, pad_end // tb - 1, -1)
    tail_blk = jnp.where(n_used + experts < n_blocks, n_used + experts, -1)
    zero_blocks = jnp.concatenate([last_blk, tail_blk]).astype(jnp.int32).reshape(1, 2 * N_EXPERTS)
    xs = _dispatch(dest.reshape(n_tok // DISPATCH_TILE, 1, DISPATCH_TILE * TOP_K), zero_blocks, xp,
                   n_blocks * tb)

    b_up = p['exp_b_up'][0]
    bg, bl = b_up[:, None, 0::2], b_up[:, None, 1::2]
    blk_new = jnp.concatenate([jnp.ones((1,), jnp.int32),
                               (blk_expert[1:] != blk_expert[:-1]).astype(jnp.int32)])
    ys = _experts(blk_expert, blk_src.astype(jnp.int32), n_used.reshape(1).astype(jnp.int32), blk_new, xs,
                  p['exp_w_up'][0], p['exp_w_down'][0], bg, bl, p['exp_b_down'][0][:, None, :])
    return _combine(dest.reshape(n_tok // COMBINE_TILE, 1, COMBINE_TILE * TOP_K), h, gates,
                    row(p['final_g']), ys)


def kernel(x, norm1_g, w_in, rwkv_mu, rwkv_w0, rwkv_wb, rwkv_a0, rwkv_ab, rwkv_gb, rwkv_kk, rwkv_ka, rwkv_rk, rwkv_ln_g, rwkv_ln_b, diff_lq1, diff_lk1, diff_lq2, diff_lk2, diff_subln_g, w_out, norm2_g, router_w, router_b, exp_w_up, exp_b_up, exp_w_down, exp_b_down, final_g):
    p = dict(rwkv_mu=rwkv_mu, rwkv_w0=rwkv_w0, rwkv_wb=rwkv_wb, rwkv_a0=rwkv_a0, rwkv_ab=rwkv_ab,
             rwkv_gb=rwkv_gb, rwkv_kk=rwkv_kk, rwkv_ka=rwkv_ka, rwkv_rk=rwkv_rk,
             rwkv_ln_g=rwkv_ln_g, rwkv_ln_b=rwkv_ln_b, diff_lq1=diff_lq1, diff_lk1=diff_lk1,
             diff_lq2=diff_lq2, diff_lk2=diff_lk2, diff_subln_g=diff_subln_g, w_out=w_out,
             norm2_g=norm2_g, router_w=router_w, router_b=router_b, exp_w_up=exp_w_up,
             exp_b_up=exp_b_up, exp_w_down=exp_w_down, exp_b_down=exp_b_down, final_g=final_g)
    bsz, seq, d = x.shape
    n_tok = bsz * seq
    x2d = x.reshape(n_tok, d)
    proj = _inproj(x2d, norm1_g, w_in[0].astype(BF16)).reshape(bsz, seq, IN_COLS)
    y_rwkv = _rwkv_stage(proj, p).reshape(n_tok, RWKV_WIDTH)
    y_diff = _diff_stage(proj, p).reshape(n_tok, DIFF_WIDTH)
    return _moe_stage(x2d, y_rwkv, y_diff, p).reshape(bsz, seq, d)
```

```python
import math

import jax
import jax.numpy as jnp
from jax import lax
from jax.experimental import pallas as pl
from jax.experimental.pallas import tpu as pltpu

F32 = jnp.float32
BF16 = jnp.bfloat16

D_MODEL = 1024
RWKV_WIDTH = 512
RWKV_HEAD = 64
DECAY_LORA = 64
ICL_LORA = 64
GATE_LORA = 128
GN_EPS = 64e-5
DIFF_WIDTH = 512
DIFF_HEAD = 64
DIFF_HEADS = 4
DIFF_VDIM = 128
RWKV_COLS = 3 * RWKV_WIDTH + DECAY_LORA + ICL_LORA + GATE_LORA
IN_COLS = RWKV_COLS + 3 * DIFF_WIDTH
N_EXPERTS = 32
TOP_K = 4
D_FF = 1024
SWIGLU_ALPHA = 1.702
SWIGLU_LIMIT = 7.0
NORM_EPS = 1e-5
LAMBDA_INIT = 0.8 - 0.6 * math.exp(-0.0)

LANES = 128
VMEM_LIMIT_BYTES = 56 * 1024 * 1024

ROW_TILE = 512
RWKV_CHUNK = 64
RWKV_GROUP = 4
GROUP_LANES = RWKV_GROUP * RWKV_HEAD
PREP_UNROLL = 2


def _dot(a, b):
    return jnp.dot(a, b, preferred_element_type=F32)


def _dot_nt(a, b):
    return lax.dot_general(a, b, (((1,), (1,)), ((), ())), preferred_element_type=F32)


def _dot_tn(a, b):
    return lax.dot_general(a, b, (((0,), (0,)), ((), ())), preferred_element_type=F32)


def _split2(x):
    hi = x.astype(BF16)
    lo = (x - hi.astype(F32)).astype(BF16)
    return hi, lo


def _interleave(*stages):
    live = list(stages)
    while live:
        for gen in list(live):
            if next(gen, StopIteration) is StopIteration:
                live.remove(gen)


def _rms(x, g):
    return x * lax.rsqrt(jnp.mean(x * x, axis=-1, keepdims=True) + NORM_EPS) * g


def _inproj_kernel(x_ref, g_ref, w_ref, o_ref):
    xn = _rms(x_ref[...], g_ref[...])
    o_ref[...] = _dot(xn.astype(BF16), w_ref[...])


def _inproj(x2d, g, w_bf16):
    n_tok = x2d.shape[0]
    return pl.pallas_call(
        _inproj_kernel,
        grid=(n_tok // ROW_TILE,),
        in_specs=[
            pl.BlockSpec((ROW_TILE, D_MODEL), lambda i: (i, 0)),
            pl.BlockSpec((1, D_MODEL), lambda i: (0, 0)),
            pl.BlockSpec((D_MODEL, IN_COLS), lambda i: (0, 0)),
        ],
        out_specs=pl.BlockSpec((ROW_TILE, IN_COLS), lambda i: (i, 0)),
        out_shape=jax.ShapeDtypeStruct((n_tok, IN_COLS), F32),
        compiler_params=pltpu.CompilerParams(
            dimension_semantics=("arbitrary",), vmem_limit_bytes=VMEM_LIMIT_BYTES),
        name="inproj",
    )(x2d, g, w_bf16)


def _head_sum(x, bd_ones):
    xb = x.astype(BF16)
    return jnp.concatenate(
        [_dot(xb[:, g * GROUP_LANES:(g + 1) * GROUP_LANES], bd_ones)
         for g in range(RWKV_WIDTH // GROUP_LANES)], axis=1)


def _rwkv_kernel(p_ref, mu_ref, w0_ref, wb_ref, a0_ref, ab_ref, gb_ref, kkw_ref, ka_ref,
                 rk_ref, lng_ref, lnb_ref, o_ref,
                 carry_ref, state_ref, r_s, k_s, v_s, kk_s, b_s, lw_s, y_s, gate_s,
                 lhs_s, w0_s, yloc_s, arb_s, kb_s, dec_s):
    ts = p_ref.shape[1]
    n_chunks = ts // RWKV_CHUNK
    n_groups = RWKV_WIDTH // GROUP_LANES

    @pl.when(pl.program_id(1) == 0)
    def _():
        carry_ref[...] = jnp.zeros_like(carry_ref)
        state_ref[...] = jnp.zeros_like(state_ref)

    gi = lax.broadcasted_iota(jnp.int32, (GROUP_LANES, GROUP_LANES), 0)
    gj = lax.broadcasted_iota(jnp.int32, (GROUP_LANES, GROUP_LANES), 1)
    same_head = (gi // RWKV_HEAD) == (gj // RWKV_HEAD)
    strict = same_head & (gi % RWKV_CHUNK > gj % RWKV_CHUNK)
    incl = same_head & (gi % RWKV_CHUNK >= gj % RWKV_CHUNK)
    eye = (gi == gj).astype(F32)
    bd_ones = same_head.astype(BF16)

    pair_rows = PREP_UNROLL * RWKV_CHUNK
    first_row = lax.broadcasted_iota(jnp.int32, (pair_rows, 1), 0) == 0
    w1, w2, w3 = RWKV_WIDTH, 2 * RWKV_WIDTH, 3 * RWKV_WIDTH

    def project(cp):
        r0 = cp * pair_rows
        rows = slice(r0, r0 + pair_rows)

        def shifted(c0, c1):
            x = p_ref[0, rows, c0:c1]
            before = carry_ref[:, c0:c1] if cp == 0 else p_ref[0, r0 - 1:r0, c0:c1]
            prev = jnp.where(first_row, before, pltpu.roll(x, 1, 0))
            return x + (prev - x) * mu_ref[:, c0:c1]

        r = shifted(0, w1)
        k = shifted(w1, w2)
        v = shifted(w2, w3)
        yield
        wa_lo = shifted(w3, w3 + DECAY_LORA + ICL_LORA)
        g_lo = shifted(w3 + DECAY_LORA + ICL_LORA, RWKV_COLS)
        z = w0_ref[...] + _dot(jnp.tanh(wa_lo).astype(BF16), wb_ref[...])
        nz = -z
        softplus = jnp.maximum(nz, 0.0) + jnp.log1p(jnp.exp(-jnp.abs(nz)))
        lw_s[rows] = -jnp.exp(-softplus - 0.5)
        yield
        a = jax.nn.sigmoid(a0_ref[...] + _dot(wa_lo.astype(BF16), ab_ref[...]))
        gate_s[rows] = _dot(jax.nn.sigmoid(g_lo).astype(BF16), gb_ref[...])
        yield
        kk = k * kkw_ref[...]
        kk = kk * lax.rsqrt(jnp.maximum(_head_sum(kk * kk, bd_ones), 1e-24))
        r_s[rows] = r
        k_s[rows] = k * (1.0 + (a - 1.0) * ka_ref[...])
        v_s[rows] = v
        kk_s[rows] = kk
        b_s[rows] = kk * a

    def finish(cp):
        rows = slice(cp * pair_rows, (cp + 1) * pair_rows)
        y = y_s[rows]
        inv_n = 1.0 / RWKV_HEAD
        d = y - _head_sum(y, bd_ones) * inv_n
        yield
        var = _head_sum(d * d, bd_ones) * inv_n
        yn = d * lax.rsqrt(var + GN_EPS) * lng_ref[...] + lnb_ref[...]
        yield
        bonus = _head_sum(r_s[rows] * k_s[rows] * rk_ref[...], bd_ones) * v_s[rows]
        o_ref[0, rows] = (yn + bonus) * gate_s[rows]

    ci = lax.broadcasted_iota(jnp.int32, (RWKV_CHUNK, RWKV_CHUNK), 0)
    cj = lax.broadcasted_iota(jnp.int32, (RWKV_CHUNK, RWKV_CHUNK), 1)
    tri = (ci >= cj).astype(BF16)
    tri2 = jnp.concatenate([tri, tri], axis=1)

    def tile_heads(x):
        return jnp.concatenate([x] * RWKV_GROUP, axis=0)

    def fold_heads(x):
        out = x[0:RWKV_CHUNK]
        for i in range(1, RWKV_GROUP):
            out = out + x[i * RWKV_CHUNK:(i + 1) * RWKV_CHUNK]
        return out

    def chunk_rows(c):
        return slice(c * RWKV_CHUNK, (c + 1) * RWKV_CHUNK)

    def prepare(cp):
        units = [(cp * PREP_UNROLL + i, g) for i in range(PREP_UNROLL) for g in range(n_groups)]
        lanes = [slice(g * GROUP_LANES, (g + 1) * GROUP_LANES) for _, g in units]
        rows = [chunk_rows(c) for c, _ in units]
        load = lambda ref: [ref[rw, ln] for rw, ln in zip(rows, lanes)]
        rc, kc, vc, kkc, bc, lwc = (load(ref) for ref in (r_s, k_s, v_s, kk_s, b_s, lw_s))
        cum = [_dot(tri2, jnp.concatenate(_split2(x), axis=0)) for x in lwc]
        yield
        cum_end = [x[RWKV_CHUNK - 1:RWKV_CHUNK, :] for x in cum]
        for (c, _), ln, ce in zip(units, lanes, cum_end):
            dec_s[c, :, ln] = jnp.exp(ce)
        e_neg = [jnp.exp(-x) for x in cum]
        e_end = [jnp.exp(ce - x) for ce, x in zip(cum_end, cum)]
        kkd = [a * jnp.exp(x - lw) for a, x, lw in zip(kkc, cum, lwc)]
        rd = [a * jnp.exp(x) for a, x in zip(rc, cum)]
        spread = lambda xs: [jnp.where(same_head, tile_heads(x), 0.0).astype(BF16) for x in xs]
        kkd_s, rd_s, v_st = spread(kkd), spread(rd), spread(vc)
        rhs = [jnp.concatenate([tile_heads(b * e), tile_heads(k * e)], axis=0).astype(BF16)
               for b, k, e in zip(bc, kc, e_neg)]
        yield
        gram = [_dot_nt(jnp.concatenate([a, b], axis=0), w) for a, b, w in zip(kkd_s, rd_s, rhs)]
        yield
        gl = GROUP_LANES
        a_ab = [jnp.where(strict, x[:gl, :gl], 0.0) for x in gram]
        a_rb = [jnp.where(incl, x[gl:, :gl], 0.0).astype(BF16) for x in gram]
        a_ak = [jnp.where(strict, x[:gl, gl:], 0.0).astype(BF16) for x in gram]
        a_rk = [jnp.where(incl, x[gl:, gl:], 0.0).astype(BF16) for x in gram]
        av = [_dot(jnp.concatenate([a, b], axis=0), v) for a, b, v in zip(a_ak, a_rk, v_st)]
        yield

        pw = [x.astype(BF16) for x in a_ab]
        t_inv = [eye - x for x in a_ab]
        for _ in range(int(math.log2(RWKV_CHUNK)) - 1):
            pw = [_dot(x, x).astype(BF16) for x in pw]
            t_inv = [t + _dot(t.astype(BF16), x) for t, x in zip(t_inv, pw)]
            yield
        sol = [_dot(t.astype(BF16), jnp.concatenate([a, x[:gl].astype(BF16)], axis=1))
               for t, a, x in zip(t_inv, kkd_s, av)]
        yield
        for i, (c, g) in enumerate(units):
            lhs_s[c, g] = jnp.concatenate([sol[i][:, :gl].astype(BF16), rd_s[i]], axis=0)
            w0_s[c, g] = sol[i][:, gl:]
            yloc_s[c, g] = fold_heads(av[i][gl:])
            arb_s[c, g] = a_rb[i]
            kb_s[c, g] = jnp.concatenate([kc[i] * e_end[i], bc[i] * e_end[i]], axis=0).astype(BF16)

    def advance(c):
        rows = chunk_rows(c)
        gs = range(n_groups)
        lanes = [slice(g * GROUP_LANES, (g + 1) * GROUP_LANES) for g in gs]
        state = [state_ref[g] for g in gs]
        m1 = [_dot_nt(lhs_s[c, g], state[g].astype(BF16)) for g in gs]
        yield
        sa_st = [m1[g][:GROUP_LANES] + w0_s[c, g] for g in gs]
        y_st = [m1[g][GROUP_LANES:] - _dot(arb_s[c, g], sa_st[g].astype(BF16)) for g in gs]
        yield
        upd = [_dot_tn(jnp.concatenate([v_s[rows, lanes[g]], -fold_heads(sa_st[g])], axis=0).astype(BF16),
                       kb_s[c, g]) for g in gs]
        yield
        for g in gs:
            y_s[rows, lanes[g]] = fold_heads(y_st[g]) + yloc_s[c, g]
            state_ref[g] = state[g] * dec_s[c, :, lanes[g]] + jnp.where(same_head, upd[g], 0.0)
        yield

    def advance_chunks(cp):
        for i in range(PREP_UNROLL):
            yield from advance(cp * PREP_UNROLL + i)

    n_pairs = n_chunks // PREP_UNROLL
    _interleave(project(0))
    _interleave(prepare(0), *([project(1)] if n_pairs > 1 else []))
    for cp in range(n_pairs):
        stages = [advance_chunks(cp)]
        if cp + 1 < n_pairs:
            stages.append(prepare(cp + 1))
        if cp + 2 < n_pairs:
            stages.append(project(cp + 2))
        if cp >= 1:
            stages.append(finish(cp - 1))
        _interleave(*stages)
    _interleave(finish(n_pairs - 1))
    carry_ref[...] = p_ref[0, ts - 1:ts, :]


def _rwkv(proj3d, mu, w0, wb_pad, a0, ab_pad, gb, kkw, ka, rk, lng, lnb):
    bsz, seq, _ = proj3d.shape
    ts = ROW_TILE
    vec = lambda n: pl.BlockSpec((1, n), lambda b, s: (0, 0))
    mat = lambda m, n: pl.BlockSpec((m, n), lambda b, s: (0, 0))
    tile_f32 = pltpu.VMEM((ts, RWKV_WIDTH), F32)
    n_chunks = ts // RWKV_CHUNK
    n_groups = RWKV_WIDTH // GROUP_LANES
    return pl.pallas_call(
        _rwkv_kernel,
        grid=(bsz, seq // ts),
        in_specs=[
            pl.BlockSpec((1, ts, RWKV_COLS), lambda b, s: (b, s, 0)),
            vec(RWKV_COLS), vec(RWKV_WIDTH), mat(DECAY_LORA + ICL_LORA, RWKV_WIDTH),
            vec(RWKV_WIDTH), mat(DECAY_LORA + ICL_LORA, RWKV_WIDTH), mat(GATE_LORA, RWKV_WIDTH),
            vec(RWKV_WIDTH), vec(RWKV_WIDTH), vec(RWKV_WIDTH), vec(RWKV_WIDTH), vec(RWKV_WIDTH),
        ],
        out_specs=pl.BlockSpec((1, ts, RWKV_WIDTH), lambda b, s: (b, s, 0)),
        out_shape=jax.ShapeDtypeStruct((bsz, seq, RWKV_WIDTH), F32),
        scratch_shapes=[
            pltpu.VMEM((1, RWKV_COLS), F32),
            pltpu.VMEM((n_groups, GROUP_LANES, GROUP_LANES), F32),
            tile_f32, tile_f32, tile_f32, tile_f32, tile_f32, tile_f32, tile_f32, tile_f32,
            pltpu.VMEM((n_chunks, n_groups, 2 * GROUP_LANES, GROUP_LANES), BF16),
            pltpu.VMEM((n_chunks, n_groups, GROUP_LANES, GROUP_LANES), F32),
            pltpu.VMEM((n_chunks, n_groups, RWKV_CHUNK, GROUP_LANES), F32),
            pltpu.VMEM((n_chunks, n_groups, GROUP_LANES, GROUP_LANES), BF16),
            pltpu.VMEM((n_chunks, n_groups, 2 * RWKV_CHUNK, GROUP_LANES), BF16),
            pltpu.VMEM((n_chunks, 1, RWKV_WIDTH), F32),
        ],
        compiler_params=pltpu.CompilerParams(
            dimension_semantics=("arbitrary", "arbitrary"), vmem_limit_bytes=VMEM_LIMIT_BYTES),
        name="rwkv7",
    )(proj3d, mu, w0, wb_pad, a0, ab_pad, gb, kkw, ka, rk, lng, lnb)


def _rwkv_stage(proj3d, p):
    row = lambda t: t.reshape(1, -1)
    zeros = jnp.zeros((DECAY_LORA, RWKV_WIDTH), F32)
    wb_pad = jnp.concatenate([p['rwkv_wb'][0], zeros], axis=0).astype(BF16)
    ab_pad = jnp.concatenate([zeros, p['rwkv_ab'][0]], axis=0).astype(BF16)
    return _rwkv(proj3d, row(p['rwkv_mu'][0]), row(p['rwkv_w0'][0]), wb_pad, row(p['rwkv_a0'][0]), ab_pad,
                 p['rwkv_gb'][0].astype(BF16), row(p['rwkv_kk'][0]), row(p['rwkv_ka'][0]),
                 row(p['rwkv_rk'][0]), row(p['rwkv_ln_g'][0]), row(p['rwkv_ln_b'][0]))


ATT_TILE = 512
ATT_LOCKSTEP = 2
ATT_AUG = 2 * LANES
POS_SPLIT_BITS = 6
MASK_VALUE = -1e30


def _alibi_lanes(n, pos0, slope, key_side):
    pos = pos0 + lax.broadcasted_iota(jnp.int32, (n, LANES), 0)
    lane = lax.broadcasted_iota(jnp.int32, (n, LANES), 1)
    hi = (pos >> POS_SPLIT_BITS).astype(F32) * (slope * float(1 << POS_SPLIT_BITS))
    lo = (pos & ((1 << POS_SPLIT_BITS) - 1)).astype(F32) * slope
    if key_side:
        return jnp.where(lane == 0, hi, jnp.where(lane == 1, lo, jnp.where(lane < 4, 1.0, 0.0)))
    return jnp.where(lane < 2, 1.0, jnp.where(lane == 2, -hi, jnp.where(lane == 3, -lo, 0.0)))


def _diff_kernel(slope_ref, lq1_ref, lk1_ref, lq2_ref, lk2_ref, g_ref, *refs):
    nh = DIFF_HEADS
    q_refs, k_refs, v_refs = refs[:nh], refs[nh:2 * nh], refs[2 * nh:3 * nh]
    o_ref, ks_ref, vs_ref, m_ref, acc_ref = refs[3 * nh:]
    tq = q_refs[0].shape[1]
    seq = k_refs[0].shape[1]
    qi = pl.program_id(1)
    slopes = [slope_ref[h][:, :1] for h in range(nh)]

    @pl.when(qi == 0)
    def _():
        for h in range(nh):
            ks_ref[h, :, :LANES] = k_refs[h][0].astype(BF16)
            ks_ref[h, :, LANES:] = _alibi_lanes(seq, 0, slopes[h], True).astype(BF16)
            vs_ref[h, :, :LANES] = v_refs[h][0].astype(BF16)
            vs_ref[h, :, LANES:] = jnp.ones((seq, LANES), BF16)

    lane = lax.broadcasted_iota(jnp.int32, (1, LANES), 1)
    qa = []
    for h in range(nh):
        q = q_refs[h][0] * (DIFF_HEAD ** -0.5)
        q_terms = _alibi_lanes(tq, qi * tq, slopes[h], False)
        qa.append(jnp.concatenate(
            [jnp.concatenate([jnp.where(lane < DIFF_HEAD, q, 0.0), q_terms], axis=1),
             jnp.concatenate([jnp.where(lane >= DIFF_HEAD, q, 0.0), q_terms], axis=1)],
            axis=0).astype(BF16))
    m_ref[...] = jnp.full_like(m_ref, MASK_VALUE)
    acc_ref[...] = jnp.zeros_like(acc_ref)
    causal = (lax.broadcasted_iota(jnp.int32, (tq, tq), 0)
              >= lax.broadcasted_iota(jnp.int32, (tq, tq), 1))
    causal = jnp.concatenate([causal, causal], axis=0)

    def process(j, diagonal):
        rows = pl.ds(pl.multiple_of(j * tq, tq), tq)
        for h0 in range(0, nh, ATT_LOCKSTEP):
            hs = range(h0, h0 + ATT_LOCKSTEP)
            s = {h: _dot_nt(qa[h], ks_ref[h, rows, :]) for h in hs}
            if diagonal:
                s = {h: jnp.where(causal, s[h], MASK_VALUE) for h in hs}
            m_old = {h: m_ref[h] for h in hs}
            m_new = {h: jnp.maximum(m_old[h], jnp.max(s[h], axis=-1, keepdims=True)) for h in hs}
            pr = {h: jnp.exp(s[h] - jnp.concatenate([m_new[h]] * (tq // LANES), axis=1)).astype(BF16)
                  for h in hs}
            pv = {h: _dot(pr[h], vs_ref[h, rows, :]) for h in hs}
            for h in hs:
                alpha = jnp.exp(m_old[h] - m_new[h])
                acc_ref[h] = jnp.concatenate([alpha, alpha], axis=1) * acc_ref[h] + pv[h]
                m_ref[h] = m_new[h]

    def off_diagonal(j, carry):
        process(j, False)
        return carry

    lax.fori_loop(0, qi, off_diagonal, 0)
    process(qi, True)

    lam = (jnp.exp(jnp.sum(lq1_ref[...] * lk1_ref[...], axis=-1, keepdims=True))
           - jnp.exp(jnp.sum(lq2_ref[...] * lk2_ref[...], axis=-1, keepdims=True)) + LAMBDA_INIT)
    for h in range(nh):
        acc = acc_ref[h]
        o12 = acc[:, :DIFF_VDIM] / acc[:, DIFF_VDIM:]
        o = o12[:tq] - lam * o12[tq:]
        o_ref[0, :, h * DIFF_VDIM:(h + 1) * DIFF_VDIM] = _rms(o, g_ref[...]) * (1.0 - LAMBDA_INIT)


def _diff(proj3d, slopes, lq1, lk1, lq2, lk2, subln_g):
    bsz, seq, _ = proj3d.shape
    tq = ATT_TILE
    q_blk0 = RWKV_COLS // DIFF_VDIM
    k_blk0 = q_blk0 + DIFF_HEADS
    v_blk0 = k_blk0 + DIFF_HEADS
    nh = DIFF_HEADS
    vec = lambda n: pl.BlockSpec((1, n), lambda b, i: (0, 0))
    q_spec = lambda h: pl.BlockSpec((1, tq, DIFF_VDIM), lambda b, i: (b, i, q_blk0 + h))
    kv_spec = lambda blk: pl.BlockSpec((1, seq, DIFF_VDIM), lambda b, i: (b, 0, blk))
    return pl.pallas_call(
        _diff_kernel,
        grid=(bsz, seq // tq),
        in_specs=([pl.BlockSpec((nh, 1, LANES), lambda b, i: (0, 0, 0)),
                   vec(DIFF_HEAD), vec(DIFF_HEAD), vec(DIFF_HEAD), vec(DIFF_HEAD), vec(DIFF_VDIM)]
                  + [q_spec(h) for h in range(nh)]
                  + [kv_spec(k_blk0 + h) for h in range(nh)]
                  + [kv_spec(v_blk0 + h) for h in range(nh)]),
        out_specs=pl.BlockSpec((1, tq, DIFF_WIDTH), lambda b, i: (b, i, 0)),
        out_shape=jax.ShapeDtypeStruct((bsz, seq, DIFF_WIDTH), F32),
        scratch_shapes=[pltpu.VMEM((nh, seq, ATT_AUG), BF16), pltpu.VMEM((nh, seq, ATT_AUG), BF16),
                        pltpu.VMEM((nh, 2 * tq, LANES), F32), pltpu.VMEM((nh, 2 * tq, ATT_AUG), F32)],
        compiler_params=pltpu.CompilerParams(
            dimension_semantics=("arbitrary", "arbitrary"), vmem_limit_bytes=VMEM_LIMIT_BYTES),
        name="diffattn",
    )(slopes, lq1, lk1, lq2, lk2, subln_g, *([proj3d] * (3 * nh)))


def _diff_stage(proj3d, p):
    row = lambda t: t.reshape(1, -1)
    slopes = 2.0 ** (-8.0 * (jnp.arange(DIFF_HEADS, dtype=F32) + 1.0) / DIFF_HEADS)
    slopes = jnp.broadcast_to(slopes[:, None, None], (DIFF_HEADS, 1, LANES))
    return _diff(proj3d, slopes, row(p['diff_lq1'][0]), row(p['diff_lk1'][0]), row(p['diff_lq2'][0]),
                 row(p['diff_lk2'][0]), row(p['diff_subln_g'][0]))


PACK_COLS = D_MODEL // 2
ROUTER_LOCKSTEP = 2
PACK_CHUNKS = PACK_COLS // LANES
OUT_CHUNKS = PACK_CHUNKS


def _pack_bf16_pairs(x):
    xb = x.astype(BF16).astype(F32)
    lo = lax.bitcast_convert_type(xb[:, :PACK_COLS], jnp.uint32) >> 16
    hi = lax.bitcast_convert_type(xb[:, PACK_COLS:], jnp.uint32)
    return lo | hi


def _unpack_pairs_f32(xp):
    lo = lax.bitcast_convert_type(xp << 16, F32)
    hi = lax.bitcast_convert_type(xp & jnp.uint32(0xFFFF0000), F32)
    return jnp.concatenate([lo, hi], axis=1)


def _unpack_bf16_pairs(xp):
    return _unpack_pairs_f32(xp).astype(BF16)


def _store_row_chunks(ref, value, first_row=0):
    n_rows, width = value.shape
    n_chunks = width // LANES
    for c in range(n_chunks):
        ref[pl.ds(first_row * n_chunks + c, n_rows, stride=n_chunks), :] = value[:, c * LANES:(c + 1) * LANES]


def _load_row_chunks(ref, n_rows, n_chunks, first_row=0):
    return jnp.concatenate(
        [ref[pl.ds(first_row * n_chunks + c, n_rows, stride=n_chunks), :] for c in range(n_chunks)], axis=1)


def _row_chunks(ref, row, n_chunks):
    return ref.at[pl.ds(pl.multiple_of(row * n_chunks, n_chunks), n_chunks), :]


def _mix_router_kernel(x_ref, yr_ref, yd_ref, wt_ref, wb_ref, g_ref, rwh_ref, rwl_ref, rb_ref,
                       h_ref, xp_ref, ti_ref, gt_ref, rk_ref, cnt_ref):
    tm = x_ref.shape[0]
    sub = tm // ROUTER_LOCKSTEP

    @pl.when(pl.program_id(0) == 0)
    def _():
        cnt_ref[...] = jnp.zeros_like(cnt_ref)

    lane_e = lax.broadcasted_iota(jnp.int32, (sub, N_EXPERTS), 1)
    lane_k = lax.broadcasted_iota(jnp.int32, (sub, TOP_K), 1)
    ri = lax.broadcasted_iota(jnp.int32, (sub, sub), 0)
    rj = lax.broadcasted_iota(jnp.int32, (sub, sub), 1)
    before = (rj < ri).astype(BF16)
    seen = [cnt_ref[...]]

    def route(i):
        rows = slice(i * sub, (i + 1) * sub)
        h = (x_ref[rows] + _dot(yr_ref[rows].astype(BF16), wt_ref[...])
             + _dot(yd_ref[rows].astype(BF16), wb_ref[...]))
        h_ref[rows] = h
        yield
        xn = _rms(h, g_ref[...])
        _store_row_chunks(xp_ref, _pack_bf16_pairs(xn), first_row=i * sub)
        x_hi, x_lo = _split2(xn)
        vals = (_dot(x_hi, rwh_ref[...]) + _dot(x_hi, rwl_ref[...]) + _dot(x_lo, rwh_ref[...])
                + rb_ref[...])
        yield
        tops, sels = [], []
        top_i = jnp.zeros((sub, TOP_K), jnp.int32)
        for k in range(TOP_K):
            m = jnp.max(vals, axis=-1, keepdims=True)
            idx = jnp.min(jnp.where(vals == m, lane_e, N_EXPERTS), axis=-1, keepdims=True)
            sel = lane_e == idx
            vals = jnp.where(sel, -jnp.inf, vals)
            tops.append(m)
            sels.append(sel)
            top_i = jnp.where(lane_k == k, idx, top_i)
            yield
        ti_ref[rows] = top_i
        exps = [jnp.exp(m - tops[0]) for m in tops]
        den = exps[0] + exps[1] + exps[2] + exps[3]
        gates = jnp.zeros((sub, TOP_K), F32)
        for k in range(TOP_K):
            gates = jnp.where(lane_k == k, exps[k] / den, gates)
        gt_ref[rows] = gates
        cnt = jnp.zeros((sub, N_EXPERTS), F32)
        for sel in sels:
            cnt = cnt + sel.astype(F32)
        earlier = seen[0]
        for part in seen[1:i + 1]:
            earlier = earlier + part
        seen.append(jnp.sum(cnt, axis=0, keepdims=True))
        yield
        prefix = _dot(before, cnt.astype(BF16)) + earlier
        rank = jnp.zeros((sub, TOP_K), F32)
        for k in range(TOP_K):
            rk = jnp.sum(jnp.where(sels[k], prefix, 0.0), axis=-1, keepdims=True)
            rank = jnp.where(lane_k == k, rk, rank)
        rk_ref[rows] = rank.astype(jnp.int32)

    _interleave(*[route(i) for i in range(ROUTER_LOCKSTEP)])
    total = seen[0]
    for part in seen[1:]:
        total = total + part
    cnt_ref[...] = total


def _mix_router(x2d, yr2d, yd2d, w_top, w_bot, g2, rw_hi, rw_lo, rb):
    n_tok = x2d.shape[0]
    tm = ROW_TILE
    rows = lambda n: pl.BlockSpec((tm, n), lambda i: (i, 0))
    full = lambda m, n: pl.BlockSpec((m, n), lambda i: (0, 0))
    return pl.pallas_call(
        _mix_router_kernel,
        grid=(n_tok // tm,),
        in_specs=[rows(D_MODEL), rows(RWKV_WIDTH), rows(DIFF_WIDTH),
                  full(RWKV_WIDTH, D_MODEL), full(DIFF_WIDTH, D_MODEL), full(1, D_MODEL),
                  full(D_MODEL, N_EXPERTS), full(D_MODEL, N_EXPERTS), full(1, N_EXPERTS)],
        out_specs=[rows(D_MODEL), pl.BlockSpec((tm * PACK_CHUNKS, LANES), lambda i: (i, 0)),
                   rows(TOP_K), rows(TOP_K), rows(TOP_K),
                   full(1, N_EXPERTS)],
        out_shape=[jax.ShapeDtypeStruct((n_tok, D_MODEL), F32),
                   jax.ShapeDtypeStruct((n_tok * PACK_CHUNKS, LANES), jnp.uint32),
                   jax.ShapeDtypeStruct((n_tok, TOP_K), jnp.int32),
                   jax.ShapeDtypeStruct((n_tok, TOP_K), F32),
                   jax.ShapeDtypeStruct((n_tok, TOP_K), jnp.int32),
                   jax.ShapeDtypeStruct((1, N_EXPERTS), F32)],
        compiler_params=pltpu.CompilerParams(
            dimension_semantics=("arbitrary",), vmem_limit_bytes=VMEM_LIMIT_BYTES),
        name="mix_router",
    )(x2d, yr2d, yd2d, w_top, w_bot, g2, rw_hi, rw_lo, rb)


EXPERT_TILE = 512
DISPATCH_TILE = 4096


def _dispatch_kernel(dest_ref, zero_ref, xp_ref, xs_ref, zbuf, sem, zsem):
    tm = xp_ref.shape[0] // PACK_CHUNKS
    blk_rows = EXPERT_TILE * PACK_CHUNKS

    @pl.when(pl.program_id(0) == 0)
    def _():
        zbuf[...] = jnp.zeros_like(zbuf)
        zero_block = lambda i: pltpu.make_async_copy(
            zbuf, xs_ref.at[pl.ds(pl.multiple_of(zero_ref[0, i] * blk_rows, blk_rows), blk_rows), :], zsem)
        for i in range(zero_ref.shape[1]):
            pl.when(zero_ref[0, i] >= 0)(lambda i=i: zero_block(i).start())
        for i in range(zero_ref.shape[1]):
            pl.when(zero_ref[0, i] >= 0)(lambda i=i: zero_block(i).wait())

    def issue(t, carry):
        src = _row_chunks(xp_ref, t, PACK_CHUNKS)
        for k in range(TOP_K):
            d = dest_ref[0, 0, t * TOP_K + k]
            pltpu.make_async_copy(src, _row_chunks(xs_ref, d, PACK_CHUNKS), sem).start(priority=k % 2)
        return carry

    lax.fori_loop(0, tm, issue, 0, unroll=8)
    for k in range(TOP_K):
        pltpu.make_async_copy(xp_ref, xs_ref.at[pl.ds(0, tm * PACK_CHUNKS), :], sem).wait()


def _dispatch(dest3d, zero_blocks, xp, n_slots):
    n_tok = xp.shape[0] // PACK_CHUNKS
    tm = DISPATCH_TILE
    return pl.pallas_call(
        _dispatch_kernel,
        grid=(n_tok // tm,),
        in_specs=[pl.BlockSpec((1, 1, tm * TOP_K), lambda i: (i, 0, 0), memory_space=pltpu.SMEM),
                  pl.BlockSpec(zero_blocks.shape, lambda i: (0, 0), memory_space=pltpu.SMEM),
                  pl.BlockSpec((tm * PACK_CHUNKS, LANES), lambda i: (i, 0))],
        out_specs=pl.BlockSpec(memory_space=pl.ANY),
        out_shape=jax.ShapeDtypeStruct((n_slots * PACK_CHUNKS, LANES), jnp.uint32),
        scratch_shapes=[pltpu.VMEM((EXPERT_TILE * PACK_CHUNKS, LANES), jnp.uint32),
                        pltpu.SemaphoreType.DMA(()), pltpu.SemaphoreType.DMA(())],
        compiler_params=pltpu.CompilerParams(
            dimension_semantics=("arbitrary",), vmem_limit_bytes=VMEM_LIMIT_BYTES),
        name="dispatch",
    )(dest3d, zero_blocks, xp)


SPLIT_BLOCK = 2 * LANES


def _expert_kernel(be_ref, src_ref, nu_ref, new_ref, xs_ref, wup_ref, wdn_ref, bg_ref, bl_ref, bd_ref,
                   y_ref, wg_s, wl_s, wd_s):
    del be_ref, src_ref
    j = pl.program_id(0)
    used = j < nu_ref[0]

    @pl.when(jnp.logical_not(used))
    def _():
        y_ref[...] = jnp.zeros_like(y_ref)

    @pl.when(jnp.logical_and(used, new_ref[j] == 1))
    def _():
        src = lax.broadcasted_iota(jnp.int32, (SPLIT_BLOCK, SPLIT_BLOCK), 0)
        dst = lax.broadcasted_iota(jnp.int32, (SPLIT_BLOCK, SPLIT_BLOCK), 1)
        perm = (src == jnp.where(dst < LANES, 2 * dst, 2 * (dst - LANES) + 1)).astype(BF16)
        for c in range(2 * D_FF // SPLIT_BLOCK):
            blk = wup_ref[0, :, c * SPLIT_BLOCK:(c + 1) * SPLIT_BLOCK].astype(BF16)
            out = _dot(blk, perm).astype(BF16)
            wg_s[:, c * LANES:(c + 1) * LANES] = out[:, :LANES]
            wl_s[:, c * LANES:(c + 1) * LANES] = out[:, LANES:]
        wd_s[...] = wdn_ref[0].astype(BF16)

    @pl.when(used)
    def _():
        x = _unpack_bf16_pairs(_load_row_chunks(xs_ref, EXPERT_TILE, PACK_CHUNKS))
        glu = jnp.minimum(_dot(x, wg_s[...]) + bg_ref[0], SWIGLU_LIMIT)
        lin = jnp.clip(_dot(x, wl_s[...]) + bl_ref[0], -SWIGLU_LIMIT, SWIGLU_LIMIT)
        act = glu * jax.nn.sigmoid(SWIGLU_ALPHA * glu) * (lin + 1.0)
        _store_row_chunks(y_ref, _pack_bf16_pairs(_dot(act.astype(BF16), wd_s[...]) + bd_ref[0]))


def _experts(blk_expert, blk_src, n_used, blk_new, xs, w_up, w_down, bg, bl, bd):
    n_slots = xs.shape[0] // PACK_CHUNKS
    tb = EXPERT_TILE
    wspec = lambda k, n: pl.BlockSpec((1, k, n), lambda j, be, src, nu, new: (be[j], 0, 0))
    return pl.pallas_call(
        _expert_kernel,
        grid_spec=pltpu.PrefetchScalarGridSpec(
            num_scalar_prefetch=4,
            grid=(n_slots // tb,),
            in_specs=[pl.BlockSpec((tb * PACK_CHUNKS, LANES), lambda j, be, src, nu, new: (src[j], 0)),
                      wspec(D_MODEL, 2 * D_FF), wspec(D_FF, D_MODEL),
                      wspec(1, D_FF), wspec(1, D_FF), wspec(1, D_MODEL)],
            out_specs=pl.BlockSpec((tb * OUT_CHUNKS, LANES), lambda j, be, src, nu, new: (j, 0)),
            scratch_shapes=[pltpu.VMEM((D_MODEL, D_FF), BF16), pltpu.VMEM((D_MODEL, D_FF), BF16),
                            pltpu.VMEM((D_FF, D_MODEL), BF16)],
        ),
        out_shape=jax.ShapeDtypeStruct((n_slots * OUT_CHUNKS, LANES), jnp.uint32),
        compiler_params=pltpu.CompilerParams(
            dimension_semantics=("arbitrary",), vmem_limit_bytes=VMEM_LIMIT_BYTES),
        name="experts",
    )(blk_expert, blk_src, n_used, blk_new, xs, w_up, w_down, bg, bl, bd)


COMBINE_TILE = 1024


def _combine_kernel(dest_ref, h_ref, gt_ref, g_ref, ys_ref, o_ref, buf, sem):
    tm = h_ref.shape[0]

    def issue(t, carry):
        for k in range(TOP_K):
            d = dest_ref[0, 0, t * TOP_K + k]
            pltpu.make_async_copy(_row_chunks(ys_ref, d, OUT_CHUNKS),
                                  _row_chunks(buf, k * tm + t, OUT_CHUNKS), sem).start(priority=k % 2)
        return carry

    lax.fori_loop(0, tm, issue, 0, unroll=8)
    rows_per_k = tm * OUT_CHUNKS
    for k in range(TOP_K):
        pltpu.make_async_copy(ys_ref.at[pl.ds(0, rows_per_k), :],
                              buf.at[pl.ds(k * rows_per_k, rows_per_k), :], sem).wait()
    gates = gt_ref[...]
    out = h_ref[...]
    for k in range(TOP_K):
        out = out + gates[:, k:k + 1] * _unpack_pairs_f32(
            _load_row_chunks(buf, tm, OUT_CHUNKS, first_row=k * tm))
    o_ref[...] = _rms(out, g_ref[...])


def _combine(dest3d, h, gates, final_g, ys):
    n_tok = h.shape[0]
    tm = COMBINE_TILE
    return pl.pallas_call(
        _combine_kernel,
        grid=(n_tok // tm,),
        in_specs=[pl.BlockSpec((1, 1, tm * TOP_K), lambda i: (i, 0, 0), memory_space=pltpu.SMEM),
                  pl.BlockSpec((tm, D_MODEL), lambda i: (i, 0)),
                  pl.BlockSpec((tm, TOP_K), lambda i: (i, 0)),
                  pl.BlockSpec((1, D_MODEL), lambda i: (0, 0)),
                  pl.BlockSpec(memory_space=pl.ANY)],
        out_specs=pl.BlockSpec((tm, D_MODEL), lambda i: (i, 0)),
        out_shape=jax.ShapeDtypeStruct((n_tok, D_MODEL), F32),
        scratch_shapes=[pltpu.VMEM((TOP_K * tm * OUT_CHUNKS, LANES), jnp.uint32),
                        pltpu.SemaphoreType.DMA(())],
        compiler_params=pltpu.CompilerParams(
            dimension_semantics=("arbitrary",), vmem_limit_bytes=VMEM_LIMIT_BYTES),
        name="combine",
    )(dest3d, h, gates, final_g, ys)


def _moe_stage(x2d, yr2d, yd2d, p):
    n_tok = x2d.shape[0]
    row = lambda t: t.reshape(1, -1)
    w_out = p['w_out'][0].astype(BF16)
    rw = p['router_w'][0]
    rw_hi = rw.astype(BF16)
    rw_lo = (rw - rw_hi.astype(F32)).astype(BF16)
    h, xp, top_i, gates, rank, counts = _mix_router(
        x2d, yr2d, yd2d, w_out[:RWKV_WIDTH], w_out[RWKV_WIDTH:], row(p['norm2_g'][0]),
        rw_hi, rw_lo, row(p['router_b'][0]))

    tb = EXPERT_TILE
    n_blocks = (n_tok * TOP_K) // tb + N_EXPERTS
    counts = counts[0].astype(jnp.int32)
    padded = (counts + tb - 1) // tb * tb
    experts = jnp.arange(N_EXPERTS, dtype=jnp.int32)
    pad_end = jnp.sum(jnp.where(experts[None, :] <= experts[:, None], padded[None, :], 0), axis=1)
    pad_start = pad_end - padded
    dest = rank + jnp.sum(jnp.where(top_i[..., None] == experts, pad_start, 0), axis=-1)
    n_used = pad_end[-1] // tb
    blk_src = jnp.minimum(jnp.arange(n_blocks, dtype=jnp.int32), n_used - 1)
    blk_expert = jnp.minimum(
        jnp.sum((pad_end[None, :] <= (blk_src * tb)[:, None]).astype(jnp.int32), axis=1), N_EXPERTS - 1)

    last_blk = jnp.where(padded > 0, pad_end // tb - 1, -1)
    tail_blk = jnp.where(n_used + experts < n_blocks, n_used + experts, -1)
    zero_blocks = jnp.concatenate([last_blk, tail_blk]).astype(jnp.int32).reshape(1, 2 * N_EXPERTS)
    xs = _dispatch(dest.reshape(n_tok // DISPATCH_TILE, 1, DISPATCH_TILE * TOP_K), zero_blocks, xp,
                   n_blocks * tb)

    b_up = p['exp_b_up'][0]
    bg, bl = b_up[:, None, 0::2], b_up[:, None, 1::2]
    blk_new = jnp.concatenate([jnp.ones((1,), jnp.int32),
                               (blk_expert[1:] != blk_expert[:-1]).astype(jnp.int32)])
    ys = _experts(blk_expert, blk_src.astype(jnp.int32), n_used.reshape(1).astype(jnp.int32), blk_new, xs,
                  p['exp_w_up'][0], p['exp_w_down'][0], bg, bl, p['exp_b_down'][0][:, None, :])
    return _combine(dest.reshape(n_tok // COMBINE_TILE, 1, COMBINE_TILE * TOP_K), h, gates,
                    row(p['final_g']), ys)


def kernel(x, norm1_g, w_in, rwkv_mu, rwkv_w0, rwkv_wb, rwkv_a0, rwkv_ab, rwkv_gb, rwkv_kk, rwkv_ka, rwkv_rk, rwkv_ln_g, rwkv_ln_b, diff_lq1, diff_lk1, diff_lq2, diff_lk2, diff_subln_g, w_out, norm2_g, router_w, router_b, exp_w_up, exp_b_up, exp_w_down, exp_b_down, final_g):
    p = dict(rwkv_mu=rwkv_mu, rwkv_w0=rwkv_w0, rwkv_wb=rwkv_wb, rwkv_a0=rwkv_a0, rwkv_ab=rwkv_ab,
             rwkv_gb=rwkv_gb, rwkv_kk=rwkv_kk, rwkv_ka=rwkv_ka, rwkv_rk=rwkv_rk,
             rwkv_ln_g=rwkv_ln_g, rwkv_ln_b=rwkv_ln_b, diff_lq1=diff_lq1, diff_lk1=diff_lk1,
             diff_lq2=diff_lq2, diff_lk2=diff_lk2, diff_subln_g=diff_subln_g, w_out=w_out,
             norm2_g=norm2_g, router_w=router_w, router_b=router_b, exp_w_up=exp_w_up,
             exp_b_up=exp_b_up, exp_w_down=exp_w_down, exp_b_down=exp_b_down, final_g=final_g)
    bsz, seq, d = x.shape
    n_tok = bsz * seq
    x2d = x.reshape(n_tok, d)
    proj = _inproj(x2d, norm1_g, w_in[0].astype(BF16)).reshape(bsz, seq, IN_COLS)
    y_rwkv = _rwkv_stage(proj, p).reshape(n_tok, RWKV_WIDTH)
    y_diff = _diff_stage(proj, p).reshape(n_tok, DIFF_WIDTH)
    return _moe_stage(x2d, y_rwkv, y_diff, p).reshape(bsz, seq, d)
```
